```python
import jax
import jax.numpy as jnp
from jax import lax
import numpy as np


D_MODEL = 1024
BATCH = 8
SEQ = 8192
DEPTH = 1

GRID_W = 64
CTX_LEN = 256
D_MIX = D_MODEL
GLA_WIDTH = D_MIX // 2
GLA_HEADS = 4
GLA_DV = GLA_WIDTH // GLA_HEADS
GLA_DK = GLA_DV // 2
GLA_QK = GLA_HEADS * GLA_DK
GLA_LOWRANK = 16
GLA_TAU = 16.0
GLA_CHUNK = 64
SC_WIDTH = D_MIX - GLA_WIDTH
D_FF = 2816
EPS = 1e-6

COL_K = 0
COL_V = COL_K + GLA_QK
COL_AF = COL_V + GLA_WIDTH
COL_AB = COL_AF + GLA_LOWRANK
COL_Q = COL_AB + GLA_LOWRANK
COL_OG = COL_Q + GLA_QK
COL_SB = COL_OG + GLA_WIDTH
COL_SC = COL_SB + SC_WIDTH
COL_SX = COL_SC + SC_WIDTH
D_IN = COL_SX + SC_WIDTH

kernel_name = "hybrid_gla_shortconv_dit_layer"


def rmsnorm(x, g):
    xf = x.astype(jnp.float32)
    y = xf * lax.rsqrt(jnp.mean(xf * xf, axis=-1, keepdims=True) + EPS)
    return (y * g.astype(jnp.float32)).astype(x.dtype)


def adaln(cond, w, b):
    return jax.nn.silu(cond) @ w + b


def modulate(h, shift, scale):
    return h * (1.0 + scale) + shift


def flip(a):
    return a[:, ::-1]


def gate_logdecay(a_low, w, b):
    a = (a_low @ w + b).astype(jnp.float32)
    return (jax.nn.log_sigmoid(a) / GLA_TAU).reshape(a.shape[0], a.shape[1], GLA_HEADS, GLA_DK)


def gla_inputs(p, w_af, b_af, w_ab, b_ab):
    bsz, t, _ = p.shape
    k = p[..., COL_K:COL_V].reshape(bsz, t, GLA_HEADS, GLA_DK)
    v = p[..., COL_V:COL_AF].reshape(bsz, t, GLA_HEADS, GLA_DV)
    g_f = gate_logdecay(p[..., COL_AF:COL_AB], w_af, b_af)
    g_b = gate_logdecay(p[..., COL_AB:COL_Q], w_ab, b_ab)
    return k, v, g_f, g_b


def gla_final_state(k, v, g):
    b = jnp.cumsum(g.astype(jnp.float32), axis=1)
    w = jnp.exp(b[:, -1:] - b)
    return jnp.einsum('bthk,bthv->bhkv', k.astype(jnp.float32) * w, v.astype(jnp.float32))


def gla_chunked(q, k, v, g, s0, strict):
    bsz, t, h, dk = q.shape
    dv = v.shape[-1]
    n = t // GLA_CHUNK

    def chunks(a):
        a = a.astype(jnp.float32).reshape(bsz, n, GLA_CHUNK, h, a.shape[-1])
        return jnp.moveaxis(a, 1, 0)

    pos = jnp.arange(GLA_CHUNK)
    mask = (pos[None, :] < pos[:, None]) if strict else (pos[None, :] <= pos[:, None])
    mask = mask[None, :, :, None, None]

    def step(s, inp):
        qc, kc, vc, gc = inp
        b = jnp.cumsum(gc, axis=1)
        o_inter = jnp.einsum('bihk,bhkv->bihv', qc * jnp.exp(b), s)
        decay = jnp.exp(jnp.where(mask, b[:, :, None] - b[:, None, :], -jnp.inf))
        att = jnp.einsum('bihk,bjhk,bijhk->bijh', qc, kc, decay)
        o_intra = jnp.einsum('bijh,bjhv->bihv', att, vc)
        b_last = b[:, -1]
        s_new = s * jnp.exp(b_last)[..., None] + jnp.einsum('bjhk,bjhv->bhkv', kc * jnp.exp(b_last[:, None] - b), vc)
        return s_new, o_inter + o_intra

    _, o = lax.scan(step, s0.astype(jnp.float32), (chunks(q), chunks(k), chunks(v), chunks(g)))
    return jnp.moveaxis(o, 0, 1).reshape(bsz, t, h, dv).astype(v.dtype)


def dwconv_row(u, w, b, rows, width):
    bsz, t, ch = u.shape
    up = jnp.pad(u.reshape(bsz, rows, width, ch), ((0, 0), (0, 0), (1, 1), (0, 0)))
    y = w[0] * up[:, :, :-2] + w[1] * up[:, :, 1:-1] + w[2] * up[:, :, 2:] + b
    return y.reshape(bsz, t, ch)


def dwconv_grid(u, w, b, rows, width):
    bsz, t, ch = u.shape
    up = jnp.pad(u.reshape(bsz, rows, width, ch), ((0, 0), (1, 1), (1, 1), (0, 0)))
    y = b
    for dr in range(3):
        for dc in range(3):
            y = y + w[dr, dc] * up[:, dr:dr + rows, dc:dc + width]
    return y.reshape(bsz, t, ch)


def token_mixers(p, s_f, s_b, rows, width, w_af, b_af, w_ab, b_ab, g_head, w_sc, b_sc):
    bsz, t, _ = p.shape
    k, v, g_f, g_b = gla_inputs(p, w_af, b_af, w_ab, b_ab)
    q = p[..., COL_Q:COL_OG].reshape(bsz, t, GLA_HEADS, GLA_DK) * (GLA_DK ** -0.5)
    og = p[..., COL_OG:COL_SB]
    o_f = gla_chunked(q, k, v, g_f, s_f, False)
    o_b = flip(gla_chunked(flip(q), flip(k), flip(v), flip(g_b), s_b, True))
    o_gla = rmsnorm(o_f + o_b, g_head).reshape(bsz, t, GLA_WIDTH) * jax.nn.silu(og)
    sb = p[..., COL_SB:COL_SC]
    sc = p[..., COL_SC:COL_SX]
    sx = p[..., COL_SX:D_IN]
    o_sc = sb * dwconv_row(sc * sx, w_sc, b_sc, rows, width)
    return jnp.concatenate([o_gla, o_sc], axis=-1)


def conv_ffn(h, rows, width, w_up, w_cf, b_cf, w_down):
    u, gate = jnp.split(h @ w_up, 2, axis=-1)
    u = dwconv_grid(u, w_cf, b_cf, rows, width)
    return (jax.nn.silu(u) * gate) @ w_down


def setup_inputs(seed: int = 0) -> dict:
    key = jax.random.key(seed)
    ks = jax.random.split(key, 24)

    def nrm(k, shape, s):
        return jax.random.normal(k, shape, jnp.float32) * s

    L = DEPTH
    return {
        'x': nrm(ks[0], (BATCH, SEQ, D_MODEL), 1.0),
        'c': nrm(ks[1], (BATCH, D_MODEL), 1.0),
        'ctx': nrm(ks[2], (BATCH, CTX_LEN, D_MODEL), 1.0),
        'c_ctx': nrm(ks[3], (D_MODEL,), 1.0),
        'w_ada': nrm(ks[4], (L, D_MODEL, 6 * D_MODEL), D_MODEL ** -0.5),
        'b_ada': nrm(ks[5], (L, 6 * D_MODEL), 0.01),
        'g_pre_mix': 1.0 + nrm(ks[6], (L, D_MODEL), 0.05),
        'g_post_mix': 1.0 + nrm(ks[7], (L, D_MODEL), 0.05),
        'g_pre_ffn': 1.0 + nrm(ks[8], (L, D_MODEL), 0.05),
        'g_post_ffn': 1.0 + nrm(ks[9], (L, D_MODEL), 0.05),
        'w_in': nrm(ks[10], (L, D_MODEL, D_IN), D_MODEL ** -0.5),
        'w_af': nrm(ks[11], (L, GLA_LOWRANK, GLA_QK), GLA_LOWRANK ** -0.5),
        'b_af': nrm(ks[12], (L, GLA_QK), 0.1),
        'w_ab': nrm(ks[13], (L, GLA_LOWRANK, GLA_QK), GLA_LOWRANK ** -0.5),
        'b_ab': nrm(ks[14], (L, GLA_QK), 0.1),
        'g_head': 1.0 + nrm(ks[15], (L, GLA_DV), 0.05),
        'w_sc': nrm(ks[16], (L, 3, SC_WIDTH), 0.5),
        'b_sc': nrm(ks[17], (L, SC_WIDTH), 0.01),
        'w_out': nrm(ks[18], (L, D_MIX, D_MODEL), D_MIX ** -0.5),
        'w_up': nrm(ks[19], (L, D_MODEL, 2 * D_FF), D_MODEL ** -0.5),
        'w_cf': nrm(ks[20], (L, 3, 3, D_FF), 1.0 / 3.0),
        'b_cf': nrm(ks[21], (L, D_FF), 0.01),
        'w_down': nrm(ks[22], (L, D_FF, D_MODEL), D_FF ** -0.5),
    }


def reference(x, c, ctx, c_ctx, w_ada, b_ada, g_pre_mix, g_post_mix, g_pre_ffn, g_post_ffn,
              w_in, w_af, b_af, w_ab, b_ab, g_head, w_sc, b_sc, w_out, w_up, w_cf, b_cf, w_down):
    rows = x.shape[1] // GRID_W
    ctx_len = ctx.shape[1]
    for i in range(DEPTH):
        update_ctx = i + 1 < DEPTH
        sh1, sc1, gt1, sh2, sc2, gt2 = jnp.split(adaln(c, w_ada[i], b_ada[i])[:, None, :], 6, axis=-1)
        csh1, csc1, cgt1, csh2, csc2, cgt2 = jnp.split(adaln(c_ctx, w_ada[i], b_ada[i]), 6, axis=-1)
        gla_p = (w_af[i], b_af[i], w_ab[i], b_ab[i])

        hc = modulate(rmsnorm(ctx, g_pre_mix[i]), csh1, csc1)
        pc = hc @ (w_in[i] if update_ctx else w_in[i][:, :COL_Q])
        kc, vc, gfc, gbc = gla_inputs(pc, *gla_p)
        s_f = gla_final_state(kc, vc, gfc)
        s_b = gla_final_state(flip(kc), flip(vc), flip(gbc))

        hx = modulate(rmsnorm(x, g_pre_mix[i]), sh1, sc1)
        yx = token_mixers(hx @ w_in[i], s_f, s_b, rows, GRID_W, *gla_p, g_head[i], w_sc[i], b_sc[i])
        x = x + gt1 * rmsnorm(yx @ w_out[i], g_post_mix[i])

        hx = modulate(rmsnorm(x, g_pre_ffn[i]), sh2, sc2)
        x = x + gt2 * rmsnorm(conv_ffn(hx, rows, GRID_W, w_up[i], w_cf[i], b_cf[i], w_down[i]), g_post_ffn[i])

        if update_ctx:
            zero_state = jnp.zeros_like(s_f)
            yc = token_mixers(pc, zero_state, zero_state, 1, ctx_len, *gla_p, g_head[i], w_sc[i], b_sc[i])
            ctx = ctx + cgt1 * rmsnorm(yc @ w_out[i], g_post_mix[i])
            hc = modulate(rmsnorm(ctx, g_pre_ffn[i]), csh2, csc2)
            ctx = ctx + cgt2 * rmsnorm(conv_ffn(hc, 1, ctx_len, w_up[i], w_cf[i], b_cf[i], w_down[i]), g_post_ffn[i])
    return x
```

```python
import functools

import jax
import jax.numpy as jnp
from jax import lax
from jax.experimental import pallas as pl
from jax.experimental.pallas import tpu as pltpu

F32 = jnp.float32
BF16 = jnp.bfloat16

D_MODEL = 1024
GRID_W = 64
HEADS = 4
DK = 64
DV = 128
QK = HEADS * DK
GLA_W = HEADS * DV
LOWRANK = 16
TAU = 16.0
CHUNK = 64
SC_W = 512
D_FF = 2816
EPS = 1e-6

COL_K = 0
COL_V = COL_K + QK
COL_AF = COL_V + GLA_W
COL_AB = COL_AF + LOWRANK
COL_Q = COL_AB + LOWRANK
COL_OG = COL_Q + QK
COL_SB = COL_OG + GLA_W
COL_SC = COL_SB + SC_W
COL_SX = COL_SC + SC_W
D_IN = COL_SX + SC_W

LANES = 128
MXU_N = 256
TILE_T = 512
VMEM_LIMIT = 56 * 1024 * 1024

M_K, M_Q, M_V, M_OG, M_SB, M_SC, M_SX, M_A = 0, 256, 512, 1024, 1536, 2048, 2560, 3072
M_COLS = M_A + LANES
B_K, B_Q, B_V, B_A = 0, 256, 512, 1024
B_COLS = B_A + LANES
C_K, C_V, C_A = 0, 256, 768
C_COLS = C_A + LANES


def _dot(a, b):
    return jnp.dot(a, b, preferred_element_type=F32)


def _dot_tn(a, b):
    return lax.dot_general(a, b, (((0,), (0,)), ((), ())), preferred_element_type=F32)


def _dot_nt(a, b):
    return lax.dot_general(a, b, (((1,), (1,)), ((), ())), preferred_element_type=F32)


def _split_bf16(x):
    hi = x.astype(BF16)
    lo = (x - hi.astype(F32)).astype(BF16)
    return hi, lo


def _rms(x, g):
    ms = jnp.mean(x * x, axis=-1, keepdims=True)
    return x * lax.rsqrt(ms + EPS) * g


def _silu(x):
    return x * jax.nn.sigmoid(x)


def _log_sigmoid(a):
    return jnp.minimum(a, 0.0) - jnp.log1p(jnp.exp(-jnp.abs(a)))


def _adaln_kernel(c_ref, w_ref, b_ref, o_ref):
    s = _silu(c_ref[...])
    s_hi, s_lo = _split_bf16(s)
    w_hi, w_lo = _split_bf16(w_ref[...])
    o_ref[...] = _dot(s_hi, w_hi) + _dot(s_lo, w_hi) + _dot(s_hi, w_lo) + b_ref[...]


def _adaln(cond, w, b):
    rows, d = cond.shape
    n = w.shape[1]
    bn = 1024
    return pl.pallas_call(
        _adaln_kernel,
        grid=(n // bn,),
        in_specs=[
            pl.BlockSpec((rows, d), lambda j: (0, 0)),
            pl.BlockSpec((d, bn), lambda j: (0, j)),
            pl.BlockSpec((1, bn), lambda j: (0, j)),
        ],
        out_specs=pl.BlockSpec((rows, bn), lambda j: (0, j)),
        out_shape=jax.ShapeDtypeStruct((rows, n), F32),
        compiler_params=pltpu.CompilerParams(
            dimension_semantics=("arbitrary",), vmem_limit_bytes=VMEM_LIMIT),
        name="adaln",
    )(cond, w, b)


def _stack_heads(full):
    return jnp.concatenate(
        [full[h * DK:(h + 1) * DK, h * DV:(h + 1) * DV] for h in range(HEADS)], axis=0)


def _ctx_kernel(ctx_ref, sh_ref, sc_ref, gpre_ref, w_ref, wg_ref, bg_ref, sf_ref, sb_ref):
    n = ctx_ref.shape[1]
    hc = _rms(ctx_ref[0], gpre_ref[...]) * (1.0 + sc_ref[0]) + sh_ref[0]
    pc = _dot(hc.astype(BF16), w_ref[...])
    k = pc[:, C_K:C_K + QK]
    v = pc[:, C_V:C_V + GLA_W].astype(BF16)
    a = _dot(pc[:, C_A:C_A + LANES].astype(BF16), wg_ref[...]) + bg_ref[...]
    g = _log_sigmoid(a) * (1.0 / TAU)
    row = lax.broadcasted_iota(jnp.int32, (n, n), 0)
    col = lax.broadcasted_iota(jnp.int32, (n, n), 1)
    later = (col > row).astype(BF16)
    earlier = (col < row).astype(BF16)
    gf_hi, gf_lo = _split_bf16(g[:, :QK])
    gb_hi, gb_lo = _split_bf16(g[:, QK:])
    wf = jnp.exp(_dot(later, gf_hi) + _dot(later, gf_lo))
    wb = jnp.exp(_dot(earlier, gb_hi) + _dot(earlier, gb_lo))
    sf_ref[0] = _stack_heads(_dot_tn((k * wf).astype(BF16), v))
    sb_ref[0] = _stack_heads(_dot_tn((k * wb).astype(BF16), v))


def _ctx_states(ctx, mod3, g_pre, w_ctx, wg, bg):
    bsz, n, d = ctx.shape
    ctx_row = bsz * 6
    const = lambda shape: pl.BlockSpec(shape, lambda b: (0,) * len(shape))
    state = jax.ShapeDtypeStruct((bsz, QK, DV), F32)
    return pl.pallas_call(
        _ctx_kernel,
        grid=(bsz,),
        in_specs=[
            pl.BlockSpec((1, n, d), lambda b: (b, 0, 0)),
            pl.BlockSpec((1, 1, d), lambda b: (ctx_row + 0, 0, 0)),
            pl.BlockSpec((1, 1, d), lambda b: (ctx_row + 1, 0, 0)),
            const((1, d)),
            const((d, C_COLS)),
            const((LANES, 2 * QK)),
            const((1, 2 * QK)),
        ],
        out_specs=[pl.BlockSpec((1, QK, DV), lambda b: (b, 0, 0))] * 2,
        out_shape=[state, state],
        compiler_params=pltpu.CompilerParams(
            dimension_semantics=("arbitrary",), vmem_limit_bytes=VMEM_LIMIT),
        name="ctx_state",
    )(ctx, mod3, mod3, g_pre, w_ctx, wg, bg)


def _gla_tile(p_ref, cols, g, state, o_ref, reverse):
    col_k, col_q, col_v = cols
    n_chunks = p_ref.shape[0] // CHUNK
    ii = lax.broadcasted_iota(jnp.int32, (CHUNK, CHUNK), 0)
    jj = lax.broadcasted_iota(jnp.int32, (CHUNK, CHUNK), 1)
    if reverse:
        tri = (jj >= ii).astype(BF16)
        keep = jj > ii
        ref_row, last_row = CHUNK // 2, 0
    else:
        tri = (jj <= ii).astype(BF16)
        keep = jj <= ii
        ref_row, last_row = CHUNK // 2 - 1, CHUNK - 1
    keep4 = jnp.concatenate([keep] * HEADS, axis=0)
    lane_head = lax.broadcasted_iota(jnp.int32, (CHUNK, QK), 1) // DK
    head_mask = [lane_head == h for h in range(HEADS)]
    tile_head = lax.broadcasted_iota(jnp.int32, (CHUNK, LANES), 1) // DK
    tile_mask = [tile_head == (h % (LANES // DK)) for h in range(HEADS)]

    order = range(n_chunks - 1, -1, -1) if reverse else range(n_chunks)
    for c in order:
        rows = slice(c * CHUNK, (c + 1) * CHUNK)
        kc = p_ref[rows, col_k:col_k + QK]
        qc = p_ref[rows, col_q:col_q + QK] * (DK ** -0.5)
        vc = p_ref[rows, col_v:col_v + GLA_W].astype(BF16)
        g_hi, g_lo = _split_bf16(g[rows, :])
        b = _dot(tri, g_hi) + _dot(tri, g_lo)
        b_ref = b[ref_row:ref_row + 1, :]
        b_last = b[last_row:last_row + 1, :]
        e = b - b_ref
        q_t = qc * jnp.exp(e)
        k_t = kc * jnp.exp(-e)
        q_h = q_t * jnp.exp(b_ref)
        k_h = k_t * jnp.exp(b_last - b_ref)

        q_stack = jnp.concatenate(
            [jnp.where(head_mask[h], q_t, 0.0).astype(BF16) for h in range(HEADS)], axis=0)
        att = _dot_nt(q_stack, k_t.astype(BF16))
        att = jnp.where(keep4, att, 0.0).astype(BF16)
        s_bf = state.astype(BF16)
        outs = []
        for h in range(HEADS):
            pair = (h // 2) * LANES
            q_in = jnp.where(tile_mask[h], q_h[:, pair:pair + LANES], 0.0)
            o_h = _dot(att[h * CHUNK:(h + 1) * CHUNK, :], vc[:, h * DV:(h + 1) * DV])
            o_h = o_h + _dot(q_in.astype(BF16), s_bf[pair:pair + LANES, :])
            outs.append(o_h)
        o_ref[rows, :] = jnp.concatenate(outs, axis=1)

        upd = _stack_heads(_dot_tn(k_h.astype(BF16), vc))
        decay = jnp.broadcast_to(jnp.exp(b_last), (LANES, QK)).T[:, :DV]
        state = state * decay + upd
    return state


def _gate(p_ref, col_a, wg_ref, bg_ref):
    a = _dot(p_ref[:, col_a:col_a + LANES].astype(BF16), wg_ref[...]) + bg_ref[...]
    return _log_sigmoid(a) * (1.0 / TAU)


def _gla_bwd_kernel(x_ref, s0_ref, sh_ref, sc_ref, gpre_ref, w_ref, wg_ref, bg_ref,
                    ob_ref, p_ref, state_ref):
    @pl.when(pl.program_id(1) == 0)
    def _():
        state_ref[...] = s0_ref[0]

    hx = _rms(x_ref[0], gpre_ref[...]) * (1.0 + sc_ref[0]) + sh_ref[0]
    p_ref[...] = _dot(hx.astype(BF16), w_ref[...])
    g = _gate(p_ref, B_A, wg_ref, bg_ref)
    state_ref[...] = _gla_tile(p_ref, (B_K, B_Q, B_V), g, state_ref[...], ob_ref.at[0], reverse=True)


def _gla_bwd(x, s_b, mod3, g_pre, w_bwd, wg_b, bg_b):
    bsz, t, d = x.shape
    nt = t // TILE_T
    const = lambda shape: pl.BlockSpec(shape, lambda b, i: (0,) * len(shape),
                                       pipeline_mode=pl.Buffered(1))
    return pl.pallas_call(
        _gla_bwd_kernel,
        grid=(bsz, nt),
        in_specs=[
            pl.BlockSpec((1, TILE_T, d), lambda b, i: (b, nt - 1 - i, 0)),
            pl.BlockSpec((1, QK, DV), lambda b, i: (b, 0, 0)),
            pl.BlockSpec((1, 1, d), lambda b, i: (b * 6 + 0, 0, 0)),
            pl.BlockSpec((1, 1, d), lambda b, i: (b * 6 + 1, 0, 0)),
            const((1, d)),
            const((d, B_COLS)),
            const((LANES, QK)),
            const((1, QK)),
        ],
        out_specs=pl.BlockSpec((1, TILE_T, GLA_W), lambda b, i: (b, nt - 1 - i, 0)),
        out_shape=jax.ShapeDtypeStruct((bsz, t, GLA_W), F32),
        scratch_shapes=[pltpu.VMEM((TILE_T, B_COLS), F32), pltpu.VMEM((QK, DV), F32)],
        compiler_params=pltpu.CompilerParams(
            dimension_semantics=("arbitrary", "arbitrary"), vmem_limit_bytes=VMEM_LIMIT),
        name="gla_bwd",
    )(x, s_b, mod3, mod3, g_pre, w_bwd, wg_b, bg_b)


def _mixer_kernel(x_ref, ob_ref, s0_ref, sh_ref, sc_ref, gt_ref, gpre_ref, gpost_ref,
                  w_ref, wg_ref, bg_ref, ghead_ref, wsc_ref, bsc_ref, wout_ref,
                  x1_ref, p_ref, o_ref, state_ref):
    @pl.when(pl.program_id(1) == 0)
    def _():
        state_ref[...] = s0_ref[0]

    n = x_ref.shape[1]
    x = x_ref[0]
    hx = _rms(x, gpre_ref[...]) * (1.0 + sc_ref[0]) + sh_ref[0]
    p_ref[...] = _dot(hx.astype(BF16), w_ref[...])
    g = _gate(p_ref, M_A, wg_ref, bg_ref)
    state_ref[...] = _gla_tile(p_ref, (M_K, M_Q, M_V), g, state_ref[...], o_ref, reverse=False)

    heads = []
    for h in range(HEADS):
        cols = slice(h * DV, (h + 1) * DV)
        heads.append(_rms(o_ref[:, cols] + ob_ref[0, :, cols], ghead_ref[...]))
    o_gla = jnp.concatenate(heads, axis=1) * _silu(p_ref[:, M_OG:M_OG + GLA_W])

    u = p_ref[:, M_SC:M_SC + SC_W] * p_ref[:, M_SX:M_SX + SC_W]
    pos = lax.broadcasted_iota(jnp.int32, (n, 1), 0) % GRID_W
    left = jnp.where(pos == 0, 0.0, pltpu.roll(u, 1, 0))
    right = jnp.where(pos == GRID_W - 1, 0.0, pltpu.roll(u, n - 1, 0))
    conv = wsc_ref[0:1, :] * left + wsc_ref[1:2, :] * u + wsc_ref[2:3, :] * right + bsc_ref[...]
    o_sc = p_ref[:, M_SB:M_SB + SC_W] * conv

    yx = jnp.concatenate([o_gla, o_sc], axis=1).astype(BF16)
    x1_ref[0] = x + gt_ref[0] * _rms(_dot(yx, wout_ref[...]), gpost_ref[...])


def _mixer(x, o_b, s_f, mod3, g_pre, g_post, w_fwd, wg_f, bg_f, g_head, w_sc, b_sc, w_out):
    bsz, t, d = x.shape
    nt = t // TILE_T
    const = lambda shape: pl.BlockSpec(shape, lambda b, i: (0,) * len(shape),
                                       pipeline_mode=pl.Buffered(1))
    mod = lambda j: pl.BlockSpec((1, 1, d), lambda b, i: (b * 6 + j, 0, 0))
    return pl.pallas_call(
        _mixer_kernel,
        grid=(bsz, nt),
        in_specs=[
            pl.BlockSpec((1, TILE_T, d), lambda b, i: (b, i, 0)),
            pl.BlockSpec((1, TILE_T, GLA_W), lambda b, i: (b, i, 0)),
            pl.BlockSpec((1, QK, DV), lambda b, i: (b, 0, 0)),
            mod(0), mod(1), mod(2),
            const((1, d)), const((1, d)),
            const((d, M_COLS)),
            const((LANES, QK)), const((1, QK)),
            const((1, DV)),
            const((3, SC_W)), const((1, SC_W)),
            const((d, d)),
        ],
        out_specs=pl.BlockSpec((1, TILE_T, d), lambda b, i: (b, i, 0)),
        out_shape=jax.ShapeDtypeStruct((bsz, t, d), F32),
        scratch_shapes=[pltpu.VMEM((TILE_T, M_COLS), F32), pltpu.VMEM((TILE_T, GLA_W), F32),
                        pltpu.VMEM((QK, DV), F32)],
        compiler_params=pltpu.CompilerParams(
            dimension_semantics=("arbitrary", "arbitrary"), vmem_limit_bytes=VMEM_LIMIT),
        name="mixer",
    )(x, o_b, s_f, mod3, mod3, mod3, g_pre, g_post, w_fwd, wg_f, bg_f, g_head, w_sc, b_sc, w_out)


def _ffn_kernel(xm_ref, xp_ref, xn_ref, sh_ref, sc_ref, gt_ref, gpre_ref, gpost_ref,
                wup_ref, wcf_ref, bcf_ref, wdown_ref, out_ref, hx_ref, h_ref):
    i = pl.program_id(1)
    n = xm_ref.shape[1]
    ext = n + 2 * GRID_W

    def prep(v):
        return (_rms(v, gpre_ref[...]) * (1.0 + sc_ref[0]) + sh_ref[0]).astype(BF16)

    x1 = xm_ref[0]
    hx_ref[0:GRID_W, :] = prep(xp_ref[0])
    hx_ref[GRID_W:GRID_W + n, :] = prep(x1)
    hx_ref[GRID_W + n:ext, :] = prep(xn_ref[0])

    rowid = lax.broadcasted_iota(jnp.int32, (ext, 1), 0)
    has_prev = (i > 0).astype(F32)
    has_next = (i < pl.num_programs(1) - 1).astype(F32)
    halo = jnp.where(rowid < GRID_W, has_prev, jnp.where(rowid >= GRID_W + n, has_next, 1.0))
    pos = rowid % GRID_W

    for c in range(D_FF // MXU_N):
        cs = slice(c * MXU_N, (c + 1) * MXU_N)
        gs = slice(D_FF + c * MXU_N, D_FF + (c + 1) * MXU_N)
        u = _dot(hx_ref[...], wup_ref[:, cs]) * halo
        gate = _dot(hx_ref[GRID_W:GRID_W + n, :], wup_ref[:, gs])
        u_l = jnp.where(pos == 0, 0.0, pltpu.roll(u, 1, 0))
        u_r = jnp.where(pos == GRID_W - 1, 0.0, pltpu.roll(u, ext - 1, 0))
        y = bcf_ref[:, cs]
        for dr in range(3):
            rs = slice(dr * GRID_W, dr * GRID_W + n)
            y = (y + wcf_ref[3 * dr:3 * dr + 1, cs] * u_l[rs]
                 + wcf_ref[3 * dr + 1:3 * dr + 2, cs] * u[rs]
                 + wcf_ref[3 * dr + 2:3 * dr + 3, cs] * u_r[rs])
        h_ref[:, cs] = (_silu(y) * gate).astype(BF16)

    z = _dot(h_ref[...], wdown_ref[...])
    out_ref[0] = x1 + gt_ref[0] * _rms(z, gpost_ref[...])


def _ffn(x1, mod3, g_pre, g_post, w_up, w_cf, b_cf, w_down):
    bsz, t, d = x1.shape
    nt = t // TILE_T
    rows_per_tile = TILE_T // GRID_W
    n_rows = t // GRID_W
    const = lambda shape: pl.BlockSpec(shape, lambda b, i: (0,) * len(shape),
                                       pipeline_mode=pl.Buffered(1))
    mod = lambda j: pl.BlockSpec((1, 1, d), lambda b, i: (b * 6 + j, 0, 0))
    return pl.pallas_call(
        _ffn_kernel,
        grid=(bsz, nt),
        in_specs=[
            pl.BlockSpec((1, TILE_T, d), lambda b, i: (b, i, 0)),
            pl.BlockSpec((1, GRID_W, d),
                         lambda b, i: (b, jnp.maximum(i * rows_per_tile - 1, 0), 0)),
            pl.BlockSpec((1, GRID_W, d),
                         lambda b, i: (b, jnp.minimum((i + 1) * rows_per_tile, n_rows - 1), 0)),
            mod(3), mod(4), mod(5),
            const((1, d)), const((1, d)),
            const((d, 2 * D_FF)),
            const((9, D_FF)), const((1, D_FF)),
            const((D_FF, d)),
        ],
        out_specs=pl.BlockSpec((1, TILE_T, d), lambda b, i: (b, i, 0)),
        out_shape=jax.ShapeDtypeStruct((bsz, t, d), F32),
        scratch_shapes=[pltpu.VMEM((TILE_T + 2 * GRID_W, d), BF16),
                        pltpu.VMEM((TILE_T, D_FF), BF16)],
        compiler_params=pltpu.CompilerParams(
            dimension_semantics=("arbitrary", "arbitrary"), vmem_limit_bytes=VMEM_LIMIT),
        name="ffn",
    )(x1, x1, x1, mod3, mod3, mod3, g_pre, g_post, w_up, w_cf, b_cf, w_down)


def kernel(x, c, ctx, c_ctx, w_ada, b_ada, g_pre_mix, g_post_mix, g_pre_ffn, g_post_ffn,
           w_in, w_af, b_af, w_ab, b_ab, g_head, w_sc, b_sc, w_out, w_up, w_cf, b_cf, w_down):
    bsz, t, d = x.shape
    assert d == D_MODEL and t % TILE_T == 0 and w_in.shape[0] == 1

    cond = jnp.concatenate([c, c_ctx[None, :], jnp.zeros((16 - bsz - 1, d), F32)], axis=0)
    mod = _adaln(cond, w_ada[0], b_ada[0][None, :])
    mod3 = mod.reshape(16 * 6, 1, d)

    wi = w_in[0]
    sl = lambda a, b: wi[:, a:b]
    gate_cols = jnp.concatenate(
        [sl(COL_AF, COL_Q), jnp.zeros((d, LANES - 2 * LOWRANK), F32)], axis=1)
    w_fwd = jnp.concatenate(
        [sl(COL_K, COL_V), sl(COL_Q, COL_OG), sl(COL_V, COL_AF), sl(COL_OG, D_IN), gate_cols],
        axis=1).astype(BF16)
    w_bwd = jnp.concatenate(
        [sl(COL_K, COL_V), sl(COL_Q, COL_OG), sl(COL_V, COL_AF), gate_cols], axis=1).astype(BF16)
    w_ctx = jnp.concatenate([sl(COL_K, COL_V), sl(COL_V, COL_AF), gate_cols], axis=1).astype(BF16)
    wg = jnp.zeros((LANES, 2 * QK), F32)
    wg = wg.at[0:LOWRANK, 0:QK].set(w_af[0]).at[LOWRANK:2 * LOWRANK, QK:].set(w_ab[0]).astype(BF16)
    bg = jnp.concatenate([b_af[0], b_ab[0]])[None, :]

    s_f, s_b = _ctx_states(ctx, mod3, g_pre_mix, w_ctx, wg, bg)
    o_b = _gla_bwd(x, s_b, mod3, g_pre_mix, w_bwd, wg[:, QK:], bg[:, QK:])
    x1 = _mixer(x, o_b, s_f, mod3, g_pre_mix, g_post_mix, w_fwd, wg[:, :QK], bg[:, :QK],
                g_head, w_sc[0], b_sc, w_out[0].astype(BF16))
    return _ffn(x1, mod3, g_pre_ffn, g_post_ffn, w_up[0].astype(BF16),
                w_cf[0].reshape(9, D_FF), b_cf, w_down[0].astype(BF16))
```

```python
import functools

import jax
import jax.numpy as jnp
from jax import lax
from jax.experimental import pallas as pl
from jax.experimental.pallas import tpu as pltpu

F32 = jnp.float32
BF16 = jnp.bfloat16

D_MODEL = 1024
GRID_W = 64
HEADS = 4
DK = 64
DV = 128
QK = HEADS * DK
GLA_W = HEADS * DV
LOWRANK = 16
TAU = 16.0
CHUNK = 64
SC_W = 512
D_FF = 2816
EPS = 1e-6

COL_K = 0
COL_V = COL_K + QK
COL_AF = COL_V + GLA_W
COL_AB = COL_AF + LOWRANK
COL_Q = COL_AB + LOWRANK
COL_OG = COL_Q + QK
COL_SB = COL_OG + GLA_W
COL_SC = COL_SB + SC_W
COL_SX = COL_SC + SC_W
D_IN = COL_SX + SC_W

LANES = 128
PAIRS = HEADS // 2
PAIR_K = 2 * DK
PAIR_V = 2 * DV
STATE_SHAPE = (PAIRS, PAIR_K, PAIR_V)
MXU_N = 256
TILE_T = 512
VMEM_LIMIT = 56 * 1024 * 1024

M_K, M_Q, M_V, M_OG, M_SB, M_SC, M_SX, M_A = 0, 256, 512, 1024, 1536, 2048, 2560, 3072
M_COLS = M_A + LANES
B_K, B_Q, B_V, B_A = 0, 256, 512, 1024
B_COLS = B_A + LANES
C_K, C_V, C_A = 0, 256, 768
C_COLS = C_A + LANES


def _dot(a, b):
    return jnp.dot(a, b, preferred_element_type=F32)


def _dot_tn(a, b):
    return lax.dot_general(a, b, (((0,), (0,)), ((), ())), preferred_element_type=F32)


def _dot_nt(a, b):
    return lax.dot_general(a, b, (((1,), (1,)), ((), ())), preferred_element_type=F32)


def _split_bf16(x):
    hi = x.astype(BF16)
    lo = (x - hi.astype(F32)).astype(BF16)
    return hi, lo


def _rms(x, g):
    ms = jnp.mean(x * x, axis=-1, keepdims=True)
    return x * lax.rsqrt(ms + EPS) * g


def _silu(x):
    return x * jax.nn.sigmoid(x)


def _log_sigmoid(a):
    return jnp.minimum(a, 0.0) - jnp.log1p(jnp.exp(-jnp.abs(a)))


def _adaln_kernel(c_ref, w_ref, b_ref, o_ref):
    s = _silu(c_ref[...])
    s_hi, s_lo = _split_bf16(s)
    w_hi, w_lo = _split_bf16(w_ref[...])
    o_ref[...] = _dot(s_hi, w_hi) + _dot(s_lo, w_hi) + _dot(s_hi, w_lo) + b_ref[...]


def _adaln(cond, w, b):
    rows, d = cond.shape
    n = w.shape[1]
    bn = 1024
    return pl.pallas_call(
        _adaln_kernel,
        grid=(n // bn,),
        in_specs=[
            pl.BlockSpec((rows, d), lambda j: (0, 0)),
            pl.BlockSpec((d, bn), lambda j: (0, j)),
            pl.BlockSpec((1, bn), lambda j: (0, j)),
        ],
        out_specs=pl.BlockSpec((rows, bn), lambda j: (0, j)),
        out_shape=jax.ShapeDtypeStruct((rows, n), F32),
        compiler_params=pltpu.CompilerParams(
            dimension_semantics=("arbitrary",), vmem_limit_bytes=VMEM_LIMIT),
        name="adaln",
    )(cond, w, b)


def _pair_block_mask():
    r = lax.broadcasted_iota(jnp.int32, (PAIR_K, PAIR_V), 0) // DK
    c = lax.broadcasted_iota(jnp.int32, (PAIR_K, PAIR_V), 1) // DV
    return r == c


def _pair_states(full, mask):
    return [jnp.where(mask, full[p * PAIR_K:(p + 1) * PAIR_K, p * PAIR_V:(p + 1) * PAIR_V], 0.0)
            for p in range(PAIRS)]


def _ctx_kernel(ctx_ref, sh_ref, sc_ref, gpre_ref, w_ref, wg_ref, bg_ref, sf_ref, sb_ref):
    n = ctx_ref.shape[1]
    hc = _rms(ctx_ref[0], gpre_ref[...]) * (1.0 + sc_ref[0]) + sh_ref[0]
    pc = _dot(hc.astype(BF16), w_ref[...])
    k = pc[:, C_K:C_K + QK]
    v = pc[:, C_V:C_V + GLA_W].astype(BF16)
    a = _dot(pc[:, C_A:C_A + LANES].astype(BF16), wg_ref[...]) + bg_ref[...]
    g = _log_sigmoid(a) * (1.0 / TAU)
    row = lax.broadcasted_iota(jnp.int32, (n, n), 0)
    col = lax.broadcasted_iota(jnp.int32, (n, n), 1)
    later = (col > row).astype(BF16)
    earlier = (col < row).astype(BF16)
    gf_hi, gf_lo = _split_bf16(g[:, :QK])
    gb_hi, gb_lo = _split_bf16(g[:, QK:])
    wf = jnp.exp(_dot(later, gf_hi) + _dot(later, gf_lo))
    wb = jnp.exp(_dot(earlier, gb_hi) + _dot(earlier, gb_lo))
    mask = _pair_block_mask()
    for p, s in enumerate(_pair_states(_dot_tn((k * wf).astype(BF16), v), mask)):
        sf_ref[0, p] = s
    for p, s in enumerate(_pair_states(_dot_tn((k * wb).astype(BF16), v), mask)):
        sb_ref[0, p] = s


def _ctx_states(ctx, mod3, g_pre, w_ctx, wg, bg):
    bsz, n, d = ctx.shape
    ctx_row = bsz * 6
    const = lambda shape: pl.BlockSpec(shape, lambda b: (0,) * len(shape))
    state = jax.ShapeDtypeStruct((bsz,) + STATE_SHAPE, F32)
    return pl.pallas_call(
        _ctx_kernel,
        grid=(bsz,),
        in_specs=[
            pl.BlockSpec((1, n, d), lambda b: (b, 0, 0)),
            pl.BlockSpec((1, 1, d), lambda b: (ctx_row + 0, 0, 0)),
            pl.BlockSpec((1, 1, d), lambda b: (ctx_row + 1, 0, 0)),
            const((1, d)),
            const((d, C_COLS)),
            const((LANES, 2 * QK)),
            const((1, 2 * QK)),
        ],
        out_specs=[pl.BlockSpec((1,) + STATE_SHAPE, lambda b: (b, 0, 0, 0))] * 2,
        out_shape=[state, state],
        compiler_params=pltpu.CompilerParams(
            dimension_semantics=("arbitrary",), vmem_limit_bytes=VMEM_LIMIT),
        name="ctx_state",
    )(ctx, mod3, mod3, g_pre, w_ctx, wg, bg)


def _gla_tile(p_ref, cols, g, state_ref, o_ref, reverse):
    col_k, col_q, col_v = cols
    n_chunks = p_ref.shape[0] // CHUNK
    ii = lax.broadcasted_iota(jnp.int32, (CHUNK, CHUNK), 0)
    jj = lax.broadcasted_iota(jnp.int32, (CHUNK, CHUNK), 1)
    i_all = lax.broadcasted_iota(jnp.int32, (CHUNK, HEADS * CHUNK), 0)
    j_all = lax.broadcasted_iota(jnp.int32, (CHUNK, HEADS * CHUNK), 1) % CHUNK
    if reverse:
        tri = (jj >= ii).astype(BF16)
        keep = j_all > i_all
        ref_row, last_row = CHUNK // 2, 0
    else:
        tri = (jj <= ii).astype(BF16)
        keep = j_all <= i_all
        ref_row, last_row = CHUNK // 2 - 1, CHUNK - 1
    tri2 = jnp.concatenate([tri, tri], axis=1)
    own_head = (lax.broadcasted_iota(jnp.int32, (HEADS * CHUNK, QK), 0) // CHUNK
                == lax.broadcasted_iota(jnp.int32, (HEADS * CHUNK, QK), 1) // DK)
    block = _pair_block_mask()

    chunks = range(n_chunks)
    order = range(n_chunks - 1, -1, -1) if reverse else range(n_chunks)
    rows = [slice(c * CHUNK, (c + 1) * CHUNK) for c in chunks]
    pair_k = [slice(p * PAIR_K, (p + 1) * PAIR_K) for p in range(PAIRS)]
    pair_v = [slice(p * PAIR_V, (p + 1) * PAIR_V) for p in range(PAIRS)]

    bs = []
    for c in chunks:
        g_hi, g_lo = _split_bf16(g[rows[c], :])
        bs.append(_dot(tri2, jnp.concatenate([g_hi, g_lo], axis=0)))

    q_ts, k_stacks, q_hs, k_hs, decays = [], [], [], [], []
    for c in chunks:
        b = bs[c]
        b_ref = b[ref_row:ref_row + 1, :]
        b_last = b[last_row:last_row + 1, :]
        e = b - b_ref
        q_t = p_ref[rows[c], col_q:col_q + QK] * (DK ** -0.5) * jnp.exp(e)
        k_t = p_ref[rows[c], col_k:col_k + QK] * jnp.exp(-e)
        q_ts.append(q_t.astype(BF16))
        k_stacks.append(
            jnp.where(own_head, jnp.concatenate([k_t] * HEADS, axis=0), 0.0).astype(BF16))
        q_hs.append((q_t * jnp.exp(b_ref)).astype(BF16))
        k_hs.append((k_t * jnp.exp(b_last - b_ref)).astype(BF16))
        decays.append(jnp.exp(b_last))

    atts = [jnp.where(keep, _dot_nt(q_ts[c], k_stacks[c]), 0.0).astype(BF16) for c in chunks]

    upds, decay_rows, v_diags = [], [], []
    for c in chunks:
        u_c, d_c, v_c = [], [], []
        for p in range(PAIRS):
            v_pair = p_ref[rows[c], col_v + p * PAIR_V:col_v + (p + 1) * PAIR_V]
            v_c.append(jnp.where(block, jnp.concatenate([v_pair, v_pair], axis=0), 0.0).astype(BF16))
            u_c.append(jnp.where(block, _dot_tn(k_hs[c][:, pair_k[p]], v_pair.astype(BF16)), 0.0))
            d_c.append(jnp.broadcast_to(decays[c][:, pair_k[p]], (PAIR_V, PAIR_K)).T)
        upds.append(u_c)
        decay_rows.append(d_c)
        v_diags.append(v_c)

    seen = [[None] * PAIRS for _ in chunks]
    for p in range(PAIRS):
        s = state_ref[p]
        for c in order:
            seen[c][p] = s.astype(BF16)
            s = s * decay_rows[c][p] + upds[c][p]
        state_ref[p] = s

    for c in chunks:
        for p in range(PAIRS):
            lhs = jnp.concatenate([atts[c][:, pair_k[p]], q_hs[c][:, pair_k[p]]], axis=1)
            rhs = jnp.concatenate([v_diags[c][p], seen[c][p]], axis=0)
            o_ref[rows[c], pair_v[p]] = _dot(lhs, rhs)


def _gate(p_ref, col_a, wg_ref, bg_ref):
    a = _dot(p_ref[:, col_a:col_a + LANES].astype(BF16), wg_ref[...]) + bg_ref[...]
    return _log_sigmoid(a) * (1.0 / TAU)


def _gla_bwd_kernel(x_ref, s0_ref, sh_ref, sc_ref, gpre_ref, w_ref, wg_ref, bg_ref,
                    ob_ref, p_ref, state_ref):
    @pl.when(pl.program_id(1) == 0)
    def _():
        state_ref[...] = s0_ref[0]

    hx = _rms(x_ref[0], gpre_ref[...]) * (1.0 + sc_ref[0]) + sh_ref[0]
    p_ref[...] = _dot(hx.astype(BF16), w_ref[...])
    g = _gate(p_ref, B_A, wg_ref, bg_ref)
    _gla_tile(p_ref, (B_K, B_Q, B_V), g, state_ref, ob_ref.at[0], reverse=True)


def _gla_bwd(x, s_b, mod3, g_pre, w_bwd, wg_b, bg_b):
    bsz, t, d = x.shape
    nt = t // TILE_T
    const = lambda shape: pl.BlockSpec(shape, lambda b, i: (0,) * len(shape),
                                       pipeline_mode=pl.Buffered(1))
    return pl.pallas_call(
        _gla_bwd_kernel,
        grid=(bsz, nt),
        in_specs=[
            pl.BlockSpec((1, TILE_T, d), lambda b, i: (b, nt - 1 - i, 0)),
            pl.BlockSpec((1,) + STATE_SHAPE, lambda b, i: (b, 0, 0, 0)),
            pl.BlockSpec((1, 1, d), lambda b, i: (b * 6 + 0, 0, 0)),
            pl.BlockSpec((1, 1, d), lambda b, i: (b * 6 + 1, 0, 0)),
            const((1, d)),
            const((d, B_COLS)),
            const((LANES, QK)),
            const((1, QK)),
        ],
        out_specs=pl.BlockSpec((1, TILE_T, GLA_W), lambda b, i: (b, nt - 1 - i, 0)),
        out_shape=jax.ShapeDtypeStruct((bsz, t, GLA_W), F32),
        scratch_shapes=[pltpu.VMEM((TILE_T, B_COLS), F32), pltpu.VMEM(STATE_SHAPE, F32)],
        compiler_params=pltpu.CompilerParams(
            dimension_semantics=("arbitrary", "arbitrary"), vmem_limit_bytes=VMEM_LIMIT),
        name="gla_bwd",
    )(x, s_b, mod3, mod3, g_pre, w_bwd, wg_b, bg_b)


def _mixer_kernel(x_ref, ob_ref, s0_ref, sh_ref, sc_ref, gt_ref, gpre_ref, gpost_ref,
                  w_ref, wg_ref, bg_ref, ghead_ref, wsc_ref, bsc_ref, wout_ref,
                  x1_ref, p_ref, o_ref, state_ref):
    @pl.when(pl.program_id(1) == 0)
    def _():
        state_ref[...] = s0_ref[0]

    n = x_ref.shape[1]
    x = x_ref[0]
    hx = _rms(x, gpre_ref[...]) * (1.0 + sc_ref[0]) + sh_ref[0]
    p_ref[...] = _dot(hx.astype(BF16), w_ref[...])
    g = _gate(p_ref, M_A, wg_ref, bg_ref)
    _gla_tile(p_ref, (M_K, M_Q, M_V), g, state_ref, o_ref, reverse=False)

    heads = []
    for h in range(HEADS):
        cols = slice(h * DV, (h + 1) * DV)
        heads.append(_rms(o_ref[:, cols] + ob_ref[0, :, cols], ghead_ref[...]))
    o_gla = jnp.concatenate(heads, axis=1) * _silu(p_ref[:, M_OG:M_OG + GLA_W])

    u = p_ref[:, M_SC:M_SC + SC_W] * p_ref[:, M_SX:M_SX + SC_W]
    pos = lax.broadcasted_iota(jnp.int32, (n, 1), 0) % GRID_W
    left = jnp.where(pos == 0, 0.0, pltpu.roll(u, 1, 0))
    right = jnp.where(pos == GRID_W - 1, 0.0, pltpu.roll(u, n - 1, 0))
    conv = wsc_ref[0:1, :] * left + wsc_ref[1:2, :] * u + wsc_ref[2:3, :] * right + bsc_ref[...]
    o_sc = p_ref[:, M_SB:M_SB + SC_W] * conv

    yx = jnp.concatenate([o_gla, o_sc], axis=1).astype(BF16)
    x1_ref[0] = x + gt_ref[0] * _rms(_dot(yx, wout_ref[...]), gpost_ref[...])


def _mixer(x, o_b, s_f, mod3, g_pre, g_post, w_fwd, wg_f, bg_f, g_head, w_sc, b_sc, w_out):
    bsz, t, d = x.shape
    nt = t // TILE_T
    const = lambda shape: pl.BlockSpec(shape, lambda b, i: (0,) * len(shape),
                                       pipeline_mode=pl.Buffered(1))
    mod = lambda j: pl.BlockSpec((1, 1, d), lambda b, i: (b * 6 + j, 0, 0))
    return pl.pallas_call(
        _mixer_kernel,
        grid=(bsz, nt),
        in_specs=[
            pl.BlockSpec((1, TILE_T, d), lambda b, i: (b, i, 0)),
            pl.BlockSpec((1, TILE_T, GLA_W), lambda b, i: (b, i, 0)),
            pl.BlockSpec((1,) + STATE_SHAPE, lambda b, i: (b, 0, 0, 0)),
            mod(0), mod(1), mod(2),
            const((1, d)), const((1, d)),
            const((d, M_COLS)),
            const((LANES, QK)), const((1, QK)),
            const((1, DV)),
            const((3, SC_W)), const((1, SC_W)),
            const((d, d)),
        ],
        out_specs=pl.BlockSpec((1, TILE_T, d), lambda b, i: (b, i, 0)),
        out_shape=jax.ShapeDtypeStruct((bsz, t, d), F32),
        scratch_shapes=[pltpu.VMEM((TILE_T, M_COLS), F32), pltpu.VMEM((TILE_T, GLA_W), F32),
                        pltpu.VMEM(STATE_SHAPE, F32)],
        compiler_params=pltpu.CompilerParams(
            dimension_semantics=("arbitrary", "arbitrary"), vmem_limit_bytes=VMEM_LIMIT),
        name="mixer",
    )(x, o_b, s_f, mod3, mod3, mod3, g_pre, g_post, w_fwd, wg_f, bg_f, g_head, w_sc, b_sc, w_out)


def _ffn_kernel(xm_ref, xp_ref, xn_ref, sh_ref, sc_ref, gt_ref, gpre_ref, gpost_ref,
                wup_ref, wcf_ref, bcf_ref, wdown_ref, out_ref, hx_ref, h_ref):
    i = pl.program_id(1)
    n = xm_ref.shape[1]
    ext = n + 2 * GRID_W

    def prep(v):
        return (_rms(v, gpre_ref[...]) * (1.0 + sc_ref[0]) + sh_ref[0]).astype(BF16)

    x1 = xm_ref[0]
    hx_ref[0:GRID_W, :] = prep(xp_ref[0])
    hx_ref[GRID_W:GRID_W + n, :] = prep(x1)
    hx_ref[GRID_W + n:ext, :] = prep(xn_ref[0])

    rowid = lax.broadcasted_iota(jnp.int32, (ext, 1), 0)
    has_prev = (i > 0).astype(F32)
    has_next = (i < pl.num_programs(1) - 1).astype(F32)
    halo = jnp.where(rowid < GRID_W, has_prev, jnp.where(rowid >= GRID_W + n, has_next, 1.0))
    pos = rowid % GRID_W

    for c in range(D_FF // MXU_N):
        cs = slice(c * MXU_N, (c + 1) * MXU_N)
        gs = slice(D_FF + c * MXU_N, D_FF + (c + 1) * MXU_N)
        u = _dot(hx_ref[...], wup_ref[:, cs]) * halo
        gate = _dot(hx_ref[GRID_W:GRID_W + n, :], wup_ref[:, gs])
        u_l = jnp.where(pos == 0, 0.0, pltpu.roll(u, 1, 0))
        u_r = jnp.where(pos == GRID_W - 1, 0.0, pltpu.roll(u, ext - 1, 0))
        y = bcf_ref[:, cs]
        for dr in range(3):
            rs = slice(dr * GRID_W, dr * GRID_W + n)
            y = (y + wcf_ref[3 * dr:3 * dr + 1, cs] * u_l[rs]
                 + wcf_ref[3 * dr + 1:3 * dr + 2, cs] * u[rs]
                 + wcf_ref[3 * dr + 2:3 * dr + 3, cs] * u_r[rs])
        h_ref[:, cs] = (_silu(y) * gate).astype(BF16)

    z = _dot(h_ref[...], wdown_ref[...])
    out_ref[0] = x1 + gt_ref[0] * _rms(z, gpost_ref[...])


def _ffn(x1, mod3, g_pre, g_post, w_up, w_cf, b_cf, w_down):
    bsz, t, d = x1.shape
    nt = t // TILE_T
    rows_per_tile = TILE_T // GRID_W
    n_rows = t // GRID_W
    const = lambda shape: pl.BlockSpec(shape, lambda b, i: (0,) * len(shape),
                                       pipeline_mode=pl.Buffered(1))
    mod = lambda j: pl.BlockSpec((1, 1, d), lambda b, i: (b * 6 + j, 0, 0))
    return pl.pallas_call(
        _ffn_kernel,
        grid=(bsz, nt),
        in_specs=[
            pl.BlockSpec((1, TILE_T, d), lambda b, i: (b, i, 0)),
            pl.BlockSpec((1, GRID_W, d),
                         lambda b, i: (b, jnp.maximum(i * rows_per_tile - 1, 0), 0)),
            pl.BlockSpec((1, GRID_W, d),
                         lambda b, i: (b, jnp.minimum((i + 1) * rows_per_tile, n_rows - 1), 0)),
            mod(3), mod(4), mod(5),
            const((1, d)), const((1, d)),
            const((d, 2 * D_FF)),
            const((9, D_FF)), const((1, D_FF)),
            const((D_FF, d)),
        ],
        out_specs=pl.BlockSpec((1, TILE_T, d), lambda b, i: (b, i, 0)),
        out_shape=jax.ShapeDtypeStruct((bsz, t, d), F32),
        scratch_shapes=[pltpu.VMEM((TILE_T + 2 * GRID_W, d), BF16),
                        pltpu.VMEM((TILE_T, D_FF), BF16)],
        compiler_params=pltpu.CompilerParams(
            dimension_semantics=("arbitrary", "arbitrary"), vmem_limit_bytes=VMEM_LIMIT),
        name="ffn",
    )(x1, x1, x1, mod3, mod3, mod3, g_pre, g_post, w_up, w_cf, b_cf, w_down)


def kernel(x, c, ctx, c_ctx, w_ada, b_ada, g_pre_mix, g_post_mix, g_pre_ffn, g_post_ffn,
           w_in, w_af, b_af, w_ab, b_ab, g_head, w_sc, b_sc, w_out, w_up, w_cf, b_cf, w_down):
    bsz, t, d = x.shape
    assert d == D_MODEL and t % TILE_T == 0 and w_in.shape[0] == 1

    cond = jnp.concatenate([c, c_ctx[None, :], jnp.zeros((16 - bsz - 1, d), F32)], axis=0)
    mod = _adaln(cond, w_ada[0], b_ada[0][None, :])
    mod3 = mod.reshape(16 * 6, 1, d)

    wi = w_in[0]
    sl = lambda a, b: wi[:, a:b]
    gate_cols = jnp.concatenate(
        [sl(COL_AF, COL_Q), jnp.zeros((d, LANES - 2 * LOWRANK), F32)], axis=1)
    w_fwd = jnp.concatenate(
        [sl(COL_K, COL_V), sl(COL_Q, COL_OG), sl(COL_V, COL_AF), sl(COL_OG, D_IN), gate_cols],
        axis=1).astype(BF16)
    w_bwd = jnp.concatenate(
        [sl(COL_K, COL_V), sl(COL_Q, COL_OG), sl(COL_V, COL_AF), gate_cols], axis=1).astype(BF16)
    w_ctx = jnp.concatenate([sl(COL_K, COL_V), sl(COL_V, COL_AF), gate_cols], axis=1).astype(BF16)
    wg = jnp.zeros((LANES, 2 * QK), F32)
    wg = wg.at[0:LOWRANK, 0:QK].set(w_af[0]).at[LOWRANK:2 * LOWRANK, QK:].set(w_ab[0]).astype(BF16)
    bg = jnp.concatenate([b_af[0], b_ab[0]])[None, :]

    s_f, s_b = _ctx_states(ctx, mod3, g_pre_mix, w_ctx, wg, bg)
    o_b = _gla_bwd(x, s_b, mod3, g_pre_mix, w_bwd, wg[:, QK:], bg[:, QK:])
    x1 = _mixer(x, o_b, s_f, mod3, g_pre_mix, g_post_mix, w_fwd, wg[:, :QK], bg[:, :QK],
                g_head, w_sc[0], b_sc, w_out[0].astype(BF16))
    return _ffn(x1, mod3, g_pre_ffn, g_post_ffn, w_up[0].astype(BF16),
                w_cf[0].reshape(9, D_FF), b_cf, w_down[0].astype(BF16))
```

```python
import functools

import jax
import jax.numpy as jnp
from jax import lax
from jax.experimental import pallas as pl
from jax.experimental.pallas import tpu as pltpu

F32 = jnp.float32
BF16 = jnp.bfloat16

D_MODEL = 1024
GRID_W = 64
HEADS = 4
DK = 64
DV = 128
QK = HEADS * DK
GLA_W = HEADS * DV
LOWRANK = 16
TAU = 16.0
CHUNK = 64
SC_W = 512
D_FF = 2816
EPS = 1e-6

COL_K = 0
COL_V = COL_K + QK
COL_AF = COL_V + GLA_W
COL_AB = COL_AF + LOWRANK
COL_Q = COL_AB + LOWRANK
COL_OG = COL_Q + QK
COL_SB = COL_OG + GLA_W
COL_SC = COL_SB + SC_W
COL_SX = COL_SC + SC_W
D_IN = COL_SX + SC_W

LANES = 128
PAIRS = HEADS // 2
PAIR_K = 2 * DK
PAIR_V = 2 * DV
STATE_SHAPE = (PAIRS, PAIR_K, PAIR_V)
MXU_N = 256
TILE_T = 512
VMEM_LIMIT = 56 * 1024 * 1024

M_OG, M_SB, M_SC, M_SX = 0, 512, 1024, 1536
M_COLS = M_SX + SC_W
B_K, B_Q, B_V, B_A = 0, 256, 512, 1024
B_COLS = B_A + LANES
C_K, C_V, C_A = 0, 256, 768
C_COLS = C_A + LANES


def _dot(a, b):
    return jnp.dot(a, b, preferred_element_type=F32)


def _dot_tn(a, b):
    return lax.dot_general(a, b, (((0,), (0,)), ((), ())), preferred_element_type=F32)


def _dot_nt(a, b):
    return lax.dot_general(a, b, (((1,), (1,)), ((), ())), preferred_element_type=F32)


def _split_bf16(x):
    hi = x.astype(BF16)
    lo = (x - hi.astype(F32)).astype(BF16)
    return hi, lo


def _rms(x, g):
    ms = jnp.mean(x * x, axis=-1, keepdims=True)
    return x * lax.rsqrt(ms + EPS) * g


def _silu(x):
    return x * jax.nn.sigmoid(x)


def _log_sigmoid(a):
    return jnp.minimum(a, 0.0) - jnp.log1p(jnp.exp(-jnp.abs(a)))


def _adaln_kernel(c_ref, w_ref, b_ref, o_ref):
    s = _silu(c_ref[...])
    s_hi, s_lo = _split_bf16(s)
    w_hi, w_lo = _split_bf16(w_ref[...])
    o_ref[...] = _dot(s_hi, w_hi) + _dot(s_lo, w_hi) + _dot(s_hi, w_lo) + b_ref[...]


def _adaln(cond, w, b):
    rows, d = cond.shape
    n = w.shape[1]
    bn = 1024
    return pl.pallas_call(
        _adaln_kernel,
        grid=(n // bn,),
        in_specs=[
            pl.BlockSpec((rows, d), lambda j: (0, 0)),
            pl.BlockSpec((d, bn), lambda j: (0, j)),
            pl.BlockSpec((1, bn), lambda j: (0, j)),
        ],
        out_specs=pl.BlockSpec((rows, bn), lambda j: (0, j)),
        out_shape=jax.ShapeDtypeStruct((rows, n), F32),
        compiler_params=pltpu.CompilerParams(
            dimension_semantics=("arbitrary",), vmem_limit_bytes=VMEM_LIMIT),
        name="adaln",
    )(cond, w, b)


def _pair_block_mask():
    r = lax.broadcasted_iota(jnp.int32, (PAIR_K, PAIR_V), 0) // DK
    c = lax.broadcasted_iota(jnp.int32, (PAIR_K, PAIR_V), 1) // DV
    return r == c


def _pair_states(full, mask):
    return [jnp.where(mask, full[p * PAIR_K:(p + 1) * PAIR_K, p * PAIR_V:(p + 1) * PAIR_V], 0.0)
            for p in range(PAIRS)]


def _ctx_kernel(ctx_ref, sh_ref, sc_ref, gpre_ref, w_ref, wg_ref, bg_ref, sf_ref, sb_ref):
    n = ctx_ref.shape[1]
    hc = _rms(ctx_ref[0], gpre_ref[...]) * (1.0 + sc_ref[0]) + sh_ref[0]
    pc = _dot(hc.astype(BF16), w_ref[...])
    k = pc[:, C_K:C_K + QK]
    v = pc[:, C_V:C_V + GLA_W].astype(BF16)
    a = _dot(pc[:, C_A:C_A + LANES].astype(BF16), wg_ref[...]) + bg_ref[...]
    g = _log_sigmoid(a) * (1.0 / TAU)
    row = lax.broadcasted_iota(jnp.int32, (n, n), 0)
    col = lax.broadcasted_iota(jnp.int32, (n, n), 1)
    later = (col > row).astype(BF16)
    earlier = (col < row).astype(BF16)
    gf_hi, gf_lo = _split_bf16(g[:, :QK])
    gb_hi, gb_lo = _split_bf16(g[:, QK:])
    wf = jnp.exp(_dot(later, gf_hi) + _dot(later, gf_lo))
    wb = jnp.exp(_dot(earlier, gb_hi) + _dot(earlier, gb_lo))
    mask = _pair_block_mask()
    for p, s in enumerate(_pair_states(_dot_tn((k * wf).astype(BF16), v), mask)):
        sf_ref[0, p] = s
    for p, s in enumerate(_pair_states(_dot_tn((k * wb).astype(BF16), v), mask)):
        sb_ref[0, p] = s


def _ctx_states(ctx, mod3, g_pre, w_ctx, wg, bg):
    bsz, n, d = ctx.shape
    ctx_row = bsz * 6
    const = lambda shape: pl.BlockSpec(shape, lambda b: (0,) * len(shape))
    state = jax.ShapeDtypeStruct((bsz,) + STATE_SHAPE, F32)
    return pl.pallas_call(
        _ctx_kernel,
        grid=(bsz,),
        in_specs=[
            pl.BlockSpec((1, n, d), lambda b: (b, 0, 0)),
            pl.BlockSpec((1, 1, d), lambda b: (ctx_row + 0, 0, 0)),
            pl.BlockSpec((1, 1, d), lambda b: (ctx_row + 1, 0, 0)),
            const((1, d)),
            const((d, C_COLS)),
            const((LANES, 2 * QK)),
            const((1, 2 * QK)),
        ],
        out_specs=[pl.BlockSpec((1,) + STATE_SHAPE, lambda b: (b, 0, 0, 0))] * 2,
        out_shape=[state, state],
        compiler_params=pltpu.CompilerParams(
            dimension_semantics=("arbitrary",), vmem_limit_bytes=VMEM_LIMIT),
        name="ctx_state",
    )(ctx, mod3, mod3, g_pre, w_ctx, wg, bg)


def _gla_tile(p_ref, cols, g, state_ref, o_ref, reverse):
    col_k, col_q, col_v = cols
    n_chunks = p_ref.shape[0] // CHUNK
    ii = lax.broadcasted_iota(jnp.int32, (CHUNK, CHUNK), 0)
    jj = lax.broadcasted_iota(jnp.int32, (CHUNK, CHUNK), 1)
    i_all = lax.broadcasted_iota(jnp.int32, (CHUNK, HEADS * CHUNK), 0)
    j_all = lax.broadcasted_iota(jnp.int32, (CHUNK, HEADS * CHUNK), 1) % CHUNK
    if reverse:
        tri = (jj >= ii).astype(BF16)
        keep = j_all > i_all
        ref_row, last_row = CHUNK // 2, 0
    else:
        tri = (jj <= ii).astype(BF16)
        keep = j_all <= i_all
        ref_row, last_row = CHUNK // 2 - 1, CHUNK - 1
    tri2 = jnp.concatenate([tri, tri], axis=1)
    own_head = (lax.broadcasted_iota(jnp.int32, (HEADS * CHUNK, QK), 0) // CHUNK
                == lax.broadcasted_iota(jnp.int32, (HEADS * CHUNK, QK), 1) // DK)
    block = _pair_block_mask()

    chunks = range(n_chunks)
    order = range(n_chunks - 1, -1, -1) if reverse else range(n_chunks)
    rows = [slice(c * CHUNK, (c + 1) * CHUNK) for c in chunks]
    pair_k = [slice(p * PAIR_K, (p + 1) * PAIR_K) for p in range(PAIRS)]
    pair_v = [slice(p * PAIR_V, (p + 1) * PAIR_V) for p in range(PAIRS)]

    bs = []
    for c in chunks:
        g_hi, g_lo = _split_bf16(g[rows[c], :])
        bs.append(_dot(tri2, jnp.concatenate([g_hi, g_lo], axis=0)))

    q_ts, k_stacks, q_hs, k_hs, decays = [], [], [], [], []
    for c in chunks:
        b = bs[c]
        b_ref = b[ref_row:ref_row + 1, :]
        b_last = b[last_row:last_row + 1, :]
        e = b - b_ref
        q_t = p_ref[rows[c], col_q:col_q + QK] * (DK ** -0.5) * jnp.exp(e)
        k_t = p_ref[rows[c], col_k:col_k + QK] * jnp.exp(-e)
        q_ts.append(q_t.astype(BF16))
        k_stacks.append(
            jnp.where(own_head, jnp.concatenate([k_t] * HEADS, axis=0), 0.0).astype(BF16))
        q_hs.append((q_t * jnp.exp(b_ref)).astype(BF16))
        k_hs.append((k_t * jnp.exp(b_last - b_ref)).astype(BF16))
        decays.append(jnp.exp(b_last))

    atts = [jnp.where(keep, _dot_nt(q_ts[c], k_stacks[c]), 0.0).astype(BF16) for c in chunks]

    upds, decay_rows, v_diags = [], [], []
    for c in chunks:
        u_c, d_c, v_c = [], [], []
        for p in range(PAIRS):
            v_pair = p_ref[rows[c], col_v + p * PAIR_V:col_v + (p + 1) * PAIR_V].astype(F32)
            v_c.append(jnp.where(block, jnp.concatenate([v_pair, v_pair], axis=0), 0.0).astype(BF16))
            u_c.append(jnp.where(block, _dot_tn(k_hs[c][:, pair_k[p]], v_pair.astype(BF16)), 0.0))
            d_c.append(jnp.broadcast_to(decays[c][:, pair_k[p]], (PAIR_V, PAIR_K)).T)
        upds.append(u_c)
        decay_rows.append(d_c)
        v_diags.append(v_c)

    seen = [[None] * PAIRS for _ in chunks]
    for p in range(PAIRS):
        s = state_ref[p]
        for c in order:
            seen[c][p] = s.astype(BF16)
            s = s * decay_rows[c][p] + upds[c][p]
        state_ref[p] = s

    for c in chunks:
        for p in range(PAIRS):
            lhs = jnp.concatenate([atts[c][:, pair_k[p]], q_hs[c][:, pair_k[p]]], axis=1)
            rhs = jnp.concatenate([v_diags[c][p], seen[c][p]], axis=0)
            o_ref[rows[c], pair_v[p]] = _dot(lhs, rhs)


def _gate(a_low, wg_ref, bg_ref):
    a = _dot(a_low.astype(BF16), wg_ref[...]) + bg_ref[...]
    return _log_sigmoid(a) * (1.0 / TAU)


def _gla_bwd_kernel(x_ref, s0_ref, sh_ref, sc_ref, gpre_ref, w_ref, wg_ref, bg_ref,
                    ob_ref, kqv_ref, alow_ref, p_ref, state_ref):
    @pl.when(pl.program_id(1) == 0)
    def _():
        state_ref[...] = s0_ref[0]

    hx = _rms(x_ref[0], gpre_ref[...]) * (1.0 + sc_ref[0]) + sh_ref[0]
    p_ref[...] = _dot(hx.astype(BF16), w_ref[...])
    kqv_ref[0] = p_ref[:, 0:B_A].astype(BF16)
    alow_ref[0] = p_ref[:, B_A:B_A + LANES]
    g = _gate(p_ref[:, B_A:B_A + LANES], wg_ref, bg_ref)
    _gla_tile(p_ref, (B_K, B_Q, B_V), g, state_ref, ob_ref.at[0], reverse=True)


def _gla_bwd(x, s_b, mod3, g_pre, w_bwd, wg_b, bg_b):
    bsz, t, d = x.shape
    nt = t // TILE_T
    const = lambda shape: pl.BlockSpec(shape, lambda b, i: (0,) * len(shape),
                                       pipeline_mode=pl.Buffered(1))
    return pl.pallas_call(
        _gla_bwd_kernel,
        grid=(bsz, nt),
        in_specs=[
            pl.BlockSpec((1, TILE_T, d), lambda b, i: (b, nt - 1 - i, 0)),
            pl.BlockSpec((1,) + STATE_SHAPE, lambda b, i: (b, 0, 0, 0)),
            pl.BlockSpec((1, 1, d), lambda b, i: (b * 6 + 0, 0, 0)),
            pl.BlockSpec((1, 1, d), lambda b, i: (b * 6 + 1, 0, 0)),
            const((1, d)),
            const((d, B_COLS)),
            const((LANES, QK)),
            const((1, QK)),
        ],
        out_specs=[pl.BlockSpec((1, TILE_T, GLA_W), lambda b, i: (b, nt - 1 - i, 0)),
                   pl.BlockSpec((1, TILE_T, B_A), lambda b, i: (b, nt - 1 - i, 0)),
                   pl.BlockSpec((1, TILE_T, LANES), lambda b, i: (b, nt - 1 - i, 0))],
        out_shape=[jax.ShapeDtypeStruct((bsz, t, GLA_W), F32),
                   jax.ShapeDtypeStruct((bsz, t, B_A), BF16),
                   jax.ShapeDtypeStruct((bsz, t, LANES), F32)],
        scratch_shapes=[pltpu.VMEM((TILE_T, B_COLS), F32), pltpu.VMEM(STATE_SHAPE, F32)],
        compiler_params=pltpu.CompilerParams(
            dimension_semantics=("arbitrary", "arbitrary"), vmem_limit_bytes=VMEM_LIMIT),
        name="gla_bwd",
    )(x, s_b, mod3, mod3, g_pre, w_bwd, wg_b, bg_b)


def _mixer_kernel(x_ref, ob_ref, kqv_ref, alow_ref, s0_ref, sh_ref, sc_ref, gt_ref, gpre_ref,
                  gpost_ref, w_ref, wg_ref, bg_ref, ghead_ref, wsc_ref, bsc_ref, wout_ref,
                  x1_ref, p_ref, o_ref, state_ref):
    @pl.when(pl.program_id(1) == 0)
    def _():
        state_ref[...] = s0_ref[0]

    n = x_ref.shape[1]
    x = x_ref[0]
    hx = _rms(x, gpre_ref[...]) * (1.0 + sc_ref[0]) + sh_ref[0]
    p_ref[...] = _dot(hx.astype(BF16), w_ref[...])
    g = _gate(alow_ref[0], wg_ref, bg_ref)
    _gla_tile(kqv_ref.at[0], (B_K, B_Q, B_V), g, state_ref, o_ref, reverse=False)

    heads = []
    for h in range(HEADS):
        cols = slice(h * DV, (h + 1) * DV)
        heads.append(_rms(o_ref[:, cols] + ob_ref[0, :, cols], ghead_ref[...]))
    o_gla = jnp.concatenate(heads, axis=1) * _silu(p_ref[:, M_OG:M_OG + GLA_W])

    u = p_ref[:, M_SC:M_SC + SC_W] * p_ref[:, M_SX:M_SX + SC_W]
    pos = lax.broadcasted_iota(jnp.int32, (n, 1), 0) % GRID_W
    left = jnp.where(pos == 0, 0.0, pltpu.roll(u, 1, 0))
    right = jnp.where(pos == GRID_W - 1, 0.0, pltpu.roll(u, n - 1, 0))
    conv = wsc_ref[0:1, :] * left + wsc_ref[1:2, :] * u + wsc_ref[2:3, :] * right + bsc_ref[...]
    o_sc = p_ref[:, M_SB:M_SB + SC_W] * conv

    yx = jnp.concatenate([o_gla, o_sc], axis=1).astype(BF16)
    x1_ref[0] = x + gt_ref[0] * _rms(_dot(yx, wout_ref[...]), gpost_ref[...])


def _mixer(x, o_b, kqv, a_low, s_f, mod3, g_pre, g_post, w_fwd, wg_f, bg_f, g_head, w_sc, b_sc,
           w_out):
    bsz, t, d = x.shape
    nt = t // TILE_T
    const = lambda shape: pl.BlockSpec(shape, lambda b, i: (0,) * len(shape),
                                       pipeline_mode=pl.Buffered(1))
    mod = lambda j: pl.BlockSpec((1, 1, d), lambda b, i: (b * 6 + j, 0, 0))
    return pl.pallas_call(
        _mixer_kernel,
        grid=(bsz, nt),
        in_specs=[
            pl.BlockSpec((1, TILE_T, d), lambda b, i: (b, i, 0)),
            pl.BlockSpec((1, TILE_T, GLA_W), lambda b, i: (b, i, 0)),
            pl.BlockSpec((1, TILE_T, B_A), lambda b, i: (b, i, 0)),
            pl.BlockSpec((1, TILE_T, LANES), lambda b, i: (b, i, 0)),
            pl.BlockSpec((1,) + STATE_SHAPE, lambda b, i: (b, 0, 0, 0)),
            mod(0), mod(1), mod(2),
            const((1, d)), const((1, d)),
            const((d, M_COLS)),
            const((LANES, QK)), const((1, QK)),
            const((1, DV)),
            const((3, SC_W)), const((1, SC_W)),
            const((d, d)),
        ],
        out_specs=pl.BlockSpec((1, TILE_T, d), lambda b, i: (b, i, 0)),
        out_shape=jax.ShapeDtypeStruct((bsz, t, d), F32),
        scratch_shapes=[pltpu.VMEM((TILE_T, M_COLS), F32), pltpu.VMEM((TILE_T, GLA_W), F32),
                        pltpu.VMEM(STATE_SHAPE, F32)],
        compiler_params=pltpu.CompilerParams(
            dimension_semantics=("arbitrary", "arbitrary"), vmem_limit_bytes=VMEM_LIMIT),
        name="mixer",
    )(x, o_b, kqv, a_low, s_f, mod3, mod3, mod3, g_pre, g_post, w_fwd, wg_f, bg_f, g_head, w_sc,
      b_sc, w_out)


def _ffn_kernel(xm_ref, xn_ref, sh_ref, sc_ref, gt_ref, gpre_ref, gpost_ref,
                wup_ref, wcf_ref, bcf_ref, wdown_ref, out_ref, hx_ref, h_ref, carry_ref):
    i = pl.program_id(1)
    n = xm_ref.shape[1]
    ext = n + 2 * GRID_W
    n_chunks = D_FF // MXU_N
    col = lambda c: slice(c * MXU_N, (c + 1) * MXU_N)

    def prep(v):
        return (_rms(v, gpre_ref[...]) * (1.0 + sc_ref[0]) + sh_ref[0]).astype(BF16)

    x1 = xm_ref[0]
    hx_ref[0:n, :] = prep(x1)
    hx_ref[n:n + GRID_W, :] = prep(xn_ref[0])

    @pl.when(i == 0)
    def _():
        carry_ref[0:GRID_W, :] = jnp.zeros((GRID_W, D_FF), F32)
        for c in range(n_chunks):
            carry_ref[GRID_W:2 * GRID_W, col(c)] = _dot(hx_ref[0:GRID_W, :], wup_ref[:, col(c)])

    has_next = (i < pl.num_programs(1) - 1).astype(F32)
    below = jnp.where(lax.broadcasted_iota(jnp.int32, (n, 1), 0) >= n - GRID_W, has_next, 1.0)
    pos = lax.broadcasted_iota(jnp.int32, (ext, 1), 0) % GRID_W

    for c in range(n_chunks):
        cs = col(c)
        gs = slice(D_FF + c * MXU_N, D_FF + (c + 1) * MXU_N)
        fresh = _dot(hx_ref[GRID_W:n + GRID_W, :], wup_ref[:, cs]) * below
        gate = _dot(hx_ref[0:n, :], wup_ref[:, gs])
        u = jnp.concatenate([carry_ref[:, cs], fresh], axis=0)
        carry_ref[:, cs] = fresh[n - 2 * GRID_W:n, :]
        u_l = jnp.where(pos == 0, 0.0, pltpu.roll(u, 1, 0))
        u_r = jnp.where(pos == GRID_W - 1, 0.0, pltpu.roll(u, ext - 1, 0))
        y = bcf_ref[:, cs]
        for dr in range(3):
            rs = slice(dr * GRID_W, dr * GRID_W + n)
            y = (y + wcf_ref[3 * dr:3 * dr + 1, cs] * u_l[rs]
                 + wcf_ref[3 * dr + 1:3 * dr + 2, cs] * u[rs]
                 + wcf_ref[3 * dr + 2:3 * dr + 3, cs] * u_r[rs])
        h_ref[:, cs] = (_silu(y) * gate).astype(BF16)

    z = _dot(h_ref[...], wdown_ref[...])
    out_ref[0] = x1 + gt_ref[0] * _rms(z, gpost_ref[...])


def _ffn(x1, mod3, g_pre, g_post, w_up, w_cf, b_cf, w_down):
    bsz, t, d = x1.shape
    nt = t // TILE_T
    rows_per_tile = TILE_T // GRID_W
    n_rows = t // GRID_W
    const = lambda shape: pl.BlockSpec(shape, lambda b, i: (0,) * len(shape),
                                       pipeline_mode=pl.Buffered(1))
    mod = lambda j: pl.BlockSpec((1, 1, d), lambda b, i: (b * 6 + j, 0, 0))
    return pl.pallas_call(
        _ffn_kernel,
        grid=(bsz, nt),
        in_specs=[
            pl.BlockSpec((1, TILE_T, d), lambda b, i: (b, i, 0)),
            pl.BlockSpec((1, GRID_W, d),
                         lambda b, i: (b, jnp.minimum((i + 1) * rows_per_tile, n_rows - 1), 0)),
            mod(3), mod(4), mod(5),
            const((1, d)), const((1, d)),
            const((d, 2 * D_FF)),
            const((9, D_FF)), const((1, D_FF)),
            const((D_FF, d)),
        ],
        out_specs=pl.BlockSpec((1, TILE_T, d), lambda b, i: (b, i, 0)),
        out_shape=jax.ShapeDtypeStruct((bsz, t, d), F32),
        scratch_shapes=[pltpu.VMEM((TILE_T + GRID_W, d), BF16),
                        pltpu.VMEM((TILE_T, D_FF), BF16),
                        pltpu.VMEM((2 * GRID_W, D_FF), F32)],
        compiler_params=pltpu.CompilerParams(
            dimension_semantics=("arbitrary", "arbitrary"), vmem_limit_bytes=VMEM_LIMIT),
        name="ffn",
    )(x1, x1, mod3, mod3, mod3, g_pre, g_post, w_up, w_cf, b_cf, w_down)


def kernel(x, c, ctx, c_ctx, w_ada, b_ada, g_pre_mix, g_post_mix, g_pre_ffn, g_post_ffn,
           w_in, w_af, b_af, w_ab, b_ab, g_head, w_sc, b_sc, w_out, w_up, w_cf, b_cf, w_down):
    bsz, t, d = x.shape
    assert d == D_MODEL and t % TILE_T == 0 and w_in.shape[0] == 1

    cond = jnp.concatenate([c, c_ctx[None, :], jnp.zeros((16 - bsz - 1, d), F32)], axis=0)
    mod = _adaln(cond, w_ada[0], b_ada[0][None, :])
    mod3 = mod.reshape(16 * 6, 1, d)

    wi = w_in[0]
    sl = lambda a, b: wi[:, a:b]
    gate_cols = jnp.concatenate(
        [sl(COL_AF, COL_Q), jnp.zeros((d, LANES - 2 * LOWRANK), F32)], axis=1)
    w_fwd = sl(COL_OG, D_IN).astype(BF16)
    w_bwd = jnp.concatenate(
        [sl(COL_K, COL_V), sl(COL_Q, COL_OG), sl(COL_V, COL_AF), gate_cols], axis=1).astype(BF16)
    w_ctx = jnp.concatenate([sl(COL_K, COL_V), sl(COL_V, COL_AF), gate_cols], axis=1).astype(BF16)
    wg = jnp.zeros((LANES, 2 * QK), F32)
    wg = wg.at[0:LOWRANK, 0:QK].set(w_af[0]).at[LOWRANK:2 * LOWRANK, QK:].set(w_ab[0]).astype(BF16)
    bg = jnp.concatenate([b_af[0], b_ab[0]])[None, :]

    s_f, s_b = _ctx_states(ctx, mod3, g_pre_mix, w_ctx, wg, bg)
    o_b, kqv, a_low = _gla_bwd(x, s_b, mod3, g_pre_mix, w_bwd, wg[:, QK:], bg[:, QK:])
    x1 = _mixer(x, o_b, kqv, a_low, s_f, mod3, g_pre_mix, g_post_mix, w_fwd, wg[:, :QK],
                bg[:, :QK], g_head, w_sc[0], b_sc, w_out[0].astype(BF16))
    return _ffn(x1, mod3, g_pre_ffn, g_post_ffn, w_up[0].astype(BF16),
                w_cf[0].reshape(9, D_FF), b_cf, w_down[0].astype(BF16))
```

```python
import functools

import jax
import jax.numpy as jnp
from jax import lax
from jax.experimental import pallas as pl
from jax.experimental.pallas import tpu as pltpu

F32 = jnp.float32
BF16 = jnp.bfloat16

D_MODEL = 1024
GRID_W = 64
HEADS = 4
DK = 64
DV = 128
QK = HEADS * DK
GLA_W = HEADS * DV
LOWRANK = 16
TAU = 16.0
CHUNK = 64
SC_W = 512
D_FF = 2816
EPS = 1e-6

COL_K = 0
COL_V = COL_K + QK
COL_AF = COL_V + GLA_W
COL_AB = COL_AF + LOWRANK
COL_Q = COL_AB + LOWRANK
COL_OG = COL_Q + QK
COL_SB = COL_OG + GLA_W
COL_SC = COL_SB + SC_W
COL_SX = COL_SC + SC_W
D_IN = COL_SX + SC_W

LANES = 128
PAIRS = HEADS // 2
PAIR_K = 2 * DK
PAIR_V = 2 * DV
STATE_SHAPE = (PAIRS, PAIR_K, PAIR_V)
MXU_N = 256
TILE_SCAN = 1024
TILE_FFN = 512
VMEM_LIMIT = 56 * 1024 * 1024

M_OG, M_SB, M_SC, M_SX = 0, 512, 1024, 1536
M_COLS = M_SX + SC_W
B_K, B_Q, B_V, B_A = 0, 256, 512, 1024
B_COLS = B_A + LANES
C_K, C_V, C_A = 0, 256, 768
C_COLS = C_A + LANES


def _dot(a, b):
    return jnp.dot(a, b, preferred_element_type=F32)


def _dot_tn(a, b):
    return lax.dot_general(a, b, (((0,), (0,)), ((), ())), preferred_element_type=F32)


def _dot_nt(a, b):
    return lax.dot_general(a, b, (((1,), (1,)), ((), ())), preferred_element_type=F32)


def _split_bf16(x):
    hi = x.astype(BF16)
    lo = (x - hi.astype(F32)).astype(BF16)
    return hi, lo


def _rms(x, g):
    ms = jnp.mean(x * x, axis=-1, keepdims=True)
    return x * lax.rsqrt(ms + EPS) * g


def _silu(x):
    return x * jax.nn.sigmoid(x)


def _log_sigmoid(a):
    return jnp.minimum(a, 0.0) - jnp.log1p(jnp.exp(-jnp.abs(a)))


def _adaln_kernel(c_ref, w_ref, b_ref, o_ref):
    s = _silu(c_ref[...])
    s_hi, s_lo = _split_bf16(s)
    w_hi, w_lo = _split_bf16(w_ref[...])
    o_ref[...] = _dot(s_hi, w_hi) + _dot(s_lo, w_hi) + _dot(s_hi, w_lo) + b_ref[...]


def _adaln(cond, w, b):
    rows, d = cond.shape
    n = w.shape[1]
    bn = 1024
    return pl.pallas_call(
        _adaln_kernel,
        grid=(n // bn,),
        in_specs=[
            pl.BlockSpec((rows, d), lambda j: (0, 0)),
            pl.BlockSpec((d, bn), lambda j: (0, j)),
            pl.BlockSpec((1, bn), lambda j: (0, j)),
        ],
        out_specs=pl.BlockSpec((rows, bn), lambda j: (0, j)),
        out_shape=jax.ShapeDtypeStruct((rows, n), F32),
        compiler_params=pltpu.CompilerParams(
            dimension_semantics=("arbitrary",), vmem_limit_bytes=VMEM_LIMIT),
        name="adaln",
    )(cond, w, b)


def _pair_block_mask():
    r = lax.broadcasted_iota(jnp.int32, (PAIR_K, PAIR_V), 0) // DK
    c = lax.broadcasted_iota(jnp.int32, (PAIR_K, PAIR_V), 1) // DV
    return r == c


def _pair_states(full, mask):
    return [jnp.where(mask, full[p * PAIR_K:(p + 1) * PAIR_K, p * PAIR_V:(p + 1) * PAIR_V], 0.0)
            for p in range(PAIRS)]


def _ctx_kernel(ctx_ref, sh_ref, sc_ref, gpre_ref, w_ref, wg_ref, bg_ref, sf_ref, sb_ref):
    n = ctx_ref.shape[1]
    hc = _rms(ctx_ref[0], gpre_ref[...]) * (1.0 + sc_ref[0]) + sh_ref[0]
    pc = _dot(hc.astype(BF16), w_ref[...])
    k = pc[:, C_K:C_K + QK]
    v = pc[:, C_V:C_V + GLA_W].astype(BF16)
    a = _dot(pc[:, C_A:C_A + LANES].astype(BF16), wg_ref[...]) + bg_ref[...]
    g = _log_sigmoid(a) * (1.0 / TAU)
    row = lax.broadcasted_iota(jnp.int32, (n, n), 0)
    col = lax.broadcasted_iota(jnp.int32, (n, n), 1)
    later = (col > row).astype(BF16)
    earlier = (col < row).astype(BF16)
    gf_hi, gf_lo = _split_bf16(g[:, :QK])
    gb_hi, gb_lo = _split_bf16(g[:, QK:])
    wf = jnp.exp(_dot(later, gf_hi) + _dot(later, gf_lo))
    wb = jnp.exp(_dot(earlier, gb_hi) + _dot(earlier, gb_lo))
    mask = _pair_block_mask()
    for p, s in enumerate(_pair_states(_dot_tn((k * wf).astype(BF16), v), mask)):
        sf_ref[0, p] = s
    for p, s in enumerate(_pair_states(_dot_tn((k * wb).astype(BF16), v), mask)):
        sb_ref[0, p] = s


def _ctx_states(ctx, mod3, g_pre, w_ctx, wg, bg):
    bsz, n, d = ctx.shape
    ctx_row = bsz * 6
    const = lambda shape: pl.BlockSpec(shape, lambda b: (0,) * len(shape))
    state = jax.ShapeDtypeStruct((bsz,) + STATE_SHAPE, F32)
    return pl.pallas_call(
        _ctx_kernel,
        grid=(bsz,),
        in_specs=[
            pl.BlockSpec((1, n, d), lambda b: (b, 0, 0)),
            pl.BlockSpec((1, 1, d), lambda b: (ctx_row + 0, 0, 0)),
            pl.BlockSpec((1, 1, d), lambda b: (ctx_row + 1, 0, 0)),
            const((1, d)),
            const((d, C_COLS)),
            const((LANES, 2 * QK)),
            const((1, 2 * QK)),
        ],
        out_specs=[pl.BlockSpec((1,) + STATE_SHAPE, lambda b: (b, 0, 0, 0))] * 2,
        out_shape=[state, state],
        compiler_params=pltpu.CompilerParams(
            dimension_semantics=("arbitrary",), vmem_limit_bytes=VMEM_LIMIT),
        name="ctx_state",
    )(ctx, mod3, mod3, g_pre, w_ctx, wg, bg)


def _gla_tile(p_ref, cols, g, state_ref, o_ref, reverse):
    col_k, col_q, col_v = cols
    n_chunks = p_ref.shape[0] // CHUNK
    ii = lax.broadcasted_iota(jnp.int32, (CHUNK, CHUNK), 0)
    jj = lax.broadcasted_iota(jnp.int32, (CHUNK, CHUNK), 1)
    i_all = lax.broadcasted_iota(jnp.int32, (CHUNK, HEADS * CHUNK), 0)
    j_all = lax.broadcasted_iota(jnp.int32, (CHUNK, HEADS * CHUNK), 1) % CHUNK
    if reverse:
        tri = (jj >= ii).astype(BF16)
        keep = j_all > i_all
        ref_row, last_row = CHUNK // 2, 0
    else:
        tri = (jj <= ii).astype(BF16)
        keep = j_all <= i_all
        ref_row, last_row = CHUNK // 2 - 1, CHUNK - 1
    tri2 = jnp.concatenate([tri, tri], axis=1)
    own_head = (lax.broadcasted_iota(jnp.int32, (HEADS * CHUNK, QK), 0) // CHUNK
                == lax.broadcasted_iota(jnp.int32, (HEADS * CHUNK, QK), 1) // DK)
    block = _pair_block_mask()

    chunks = range(n_chunks)
    order = range(n_chunks - 1, -1, -1) if reverse else range(n_chunks)
    rows = [slice(c * CHUNK, (c + 1) * CHUNK) for c in chunks]
    pair_k = [slice(p * PAIR_K, (p + 1) * PAIR_K) for p in range(PAIRS)]
    pair_v = [slice(p * PAIR_V, (p + 1) * PAIR_V) for p in range(PAIRS)]

    bs = []
    for c in chunks:
        g_hi, g_lo = _split_bf16(g[rows[c], :])
        bs.append(_dot(tri2, jnp.concatenate([g_hi, g_lo], axis=0)))

    q_ts, k_stacks, q_hs, k_hs, decays = [], [], [], [], []
    for c in chunks:
        b = bs[c]
        b_ref = b[ref_row:ref_row + 1, :]
        b_last = b[last_row:last_row + 1, :]
        e = b - b_ref
        q_t = p_ref[rows[c], col_q:col_q + QK] * (DK ** -0.5) * jnp.exp(e)
        k_t = p_ref[rows[c], col_k:col_k + QK] * jnp.exp(-e)
        q_ts.append(q_t.astype(BF16))
        k_stacks.append(
            jnp.where(own_head, jnp.concatenate([k_t] * HEADS, axis=0), 0.0).astype(BF16))
        q_hs.append((q_t * jnp.exp(b_ref)).astype(BF16))
        k_hs.append((k_t * jnp.exp(b_last - b_ref)).astype(BF16))
        decays.append(jnp.exp(b_last))

    atts = [jnp.where(keep, _dot_nt(q_ts[c], k_stacks[c]), 0.0).astype(BF16) for c in chunks]

    upds, decay_rows, v_diags = [], [], []
    for c in chunks:
        u_c, d_c, v_c = [], [], []
        for p in range(PAIRS):
            v_pair = p_ref[rows[c], col_v + p * PAIR_V:col_v + (p + 1) * PAIR_V].astype(F32)
            v_c.append(jnp.where(block, jnp.concatenate([v_pair, v_pair], axis=0), 0.0).astype(BF16))
            u_c.append(jnp.where(block, _dot_tn(k_hs[c][:, pair_k[p]], v_pair.astype(BF16)), 0.0))
            d_c.append(jnp.broadcast_to(decays[c][:, pair_k[p]], (PAIR_V, PAIR_K)).T)
        upds.append(u_c)
        decay_rows.append(d_c)
        v_diags.append(v_c)

    seen = [[None] * PAIRS for _ in chunks]
    for p in range(PAIRS):
        s = state_ref[p]
        for c in order:
            seen[c][p] = s.astype(BF16)
            s = s * decay_rows[c][p] + upds[c][p]
        state_ref[p] = s

    for c in chunks:
        for p in range(PAIRS):
            lhs = jnp.concatenate([atts[c][:, pair_k[p]], q_hs[c][:, pair_k[p]]], axis=1)
            rhs = jnp.concatenate([v_diags[c][p], seen[c][p]], axis=0)
            o_ref[rows[c], pair_v[p]] = _dot(lhs, rhs)


def _gate(a_low, wg_ref, bg_ref):
    a = _dot(a_low.astype(BF16), wg_ref[...]) + bg_ref[...]
    return _log_sigmoid(a) * (1.0 / TAU)


def _gla_bwd_kernel(x_ref, s0_ref, sh_ref, sc_ref, gpre_ref, w_ref, wg_ref, bg_ref,
                    ob_ref, kqv_ref, alow_ref, p_ref, state_ref):
    @pl.when(pl.program_id(1) == 0)
    def _():
        state_ref[...] = s0_ref[0]

    hx = _rms(x_ref[0], gpre_ref[...]) * (1.0 + sc_ref[0]) + sh_ref[0]
    p_ref[...] = _dot(hx.astype(BF16), w_ref[...])
    kqv_ref[0] = p_ref[:, 0:B_A].astype(BF16)
    alow_ref[0] = p_ref[:, B_A:B_A + LANES]
    g = _gate(p_ref[:, B_A:B_A + LANES], wg_ref, bg_ref)
    _gla_tile(p_ref, (B_K, B_Q, B_V), g, state_ref, ob_ref.at[0], reverse=True)


def _gla_bwd(x, s_b, mod3, g_pre, w_bwd, wg_b, bg_b):
    bsz, t, d = x.shape
    nt = t // TILE_SCAN
    const = lambda shape: pl.BlockSpec(shape, lambda b, i: (0,) * len(shape),
                                       pipeline_mode=pl.Buffered(1))
    return pl.pallas_call(
        _gla_bwd_kernel,
        grid=(bsz, nt),
        in_specs=[
            pl.BlockSpec((1, TILE_SCAN, d), lambda b, i: (b, nt - 1 - i, 0)),
            pl.BlockSpec((1,) + STATE_SHAPE, lambda b, i: (b, 0, 0, 0)),
            pl.BlockSpec((1, 1, d), lambda b, i: (b * 6 + 0, 0, 0)),
            pl.BlockSpec((1, 1, d), lambda b, i: (b * 6 + 1, 0, 0)),
            const((1, d)),
            const((d, B_COLS)),
            const((LANES, QK)),
            const((1, QK)),
        ],
        out_specs=[pl.BlockSpec((1, TILE_SCAN, GLA_W), lambda b, i: (b, nt - 1 - i, 0)),
                   pl.BlockSpec((1, TILE_SCAN, B_A), lambda b, i: (b, nt - 1 - i, 0)),
                   pl.BlockSpec((1, TILE_SCAN, LANES), lambda b, i: (b, nt - 1 - i, 0))],
        out_shape=[jax.ShapeDtypeStruct((bsz, t, GLA_W), F32),
                   jax.ShapeDtypeStruct((bsz, t, B_A), BF16),
                   jax.ShapeDtypeStruct((bsz, t, LANES), F32)],
        scratch_shapes=[pltpu.VMEM((TILE_SCAN, B_COLS), F32), pltpu.VMEM(STATE_SHAPE, F32)],
        compiler_params=pltpu.CompilerParams(
            dimension_semantics=("arbitrary", "arbitrary"), vmem_limit_bytes=VMEM_LIMIT),
        name="gla_bwd",
    )(x, s_b, mod3, mod3, g_pre, w_bwd, wg_b, bg_b)


def _mixer_kernel(x_ref, ob_ref, kqv_ref, alow_ref, s0_ref, sh_ref, sc_ref, gt_ref, gpre_ref,
                  gpost_ref, w_ref, wg_ref, bg_ref, ghead_ref, wsc_ref, bsc_ref, wout_ref,
                  x1_ref, p_ref, o_ref, state_ref):
    @pl.when(pl.program_id(1) == 0)
    def _():
        state_ref[...] = s0_ref[0]

    n = x_ref.shape[1]
    x = x_ref[0]
    hx = _rms(x, gpre_ref[...]) * (1.0 + sc_ref[0]) + sh_ref[0]
    p_ref[...] = _dot(hx.astype(BF16), w_ref[...])
    g = _gate(alow_ref[0], wg_ref, bg_ref)
    _gla_tile(kqv_ref.at[0], (B_K, B_Q, B_V), g, state_ref, o_ref, reverse=False)

    heads = []
    for h in range(HEADS):
        cols = slice(h * DV, (h + 1) * DV)
        heads.append(_rms(o_ref[:, cols] + ob_ref[0, :, cols], ghead_ref[...]))
    o_gla = jnp.concatenate(heads, axis=1) * _silu(p_ref[:, M_OG:M_OG + GLA_W])

    u = p_ref[:, M_SC:M_SC + SC_W] * p_ref[:, M_SX:M_SX + SC_W]
    pos = lax.broadcasted_iota(jnp.int32, (n, 1), 0) % GRID_W
    left = jnp.where(pos == 0, 0.0, pltpu.roll(u, 1, 0))
    right = jnp.where(pos == GRID_W - 1, 0.0, pltpu.roll(u, n - 1, 0))
    conv = wsc_ref[0:1, :] * left + wsc_ref[1:2, :] * u + wsc_ref[2:3, :] * right + bsc_ref[...]
    o_sc = p_ref[:, M_SB:M_SB + SC_W] * conv

    yx = jnp.concatenate([o_gla, o_sc], axis=1).astype(BF16)
    x1_ref[0] = x + gt_ref[0] * _rms(_dot(yx, wout_ref[...]), gpost_ref[...])


def _mixer(x, o_b, kqv, a_low, s_f, mod3, g_pre, g_post, w_fwd, wg_f, bg_f, g_head, w_sc, b_sc,
           w_out):
    bsz, t, d = x.shape
    nt = t // TILE_SCAN
    const = lambda shape: pl.BlockSpec(shape, lambda b, i: (0,) * len(shape),
                                       pipeline_mode=pl.Buffered(1))
    mod = lambda j: pl.BlockSpec((1, 1, d), lambda b, i: (b * 6 + j, 0, 0))
    return pl.pallas_call(
        _mixer_kernel,
        grid=(bsz, nt),
        in_specs=[
            pl.BlockSpec((1, TILE_SCAN, d), lambda b, i: (b, i, 0)),
            pl.BlockSpec((1, TILE_SCAN, GLA_W), lambda b, i: (b, i, 0)),
            pl.BlockSpec((1, TILE_SCAN, B_A), lambda b, i: (b, i, 0)),
            pl.BlockSpec((1, TILE_SCAN, LANES), lambda b, i: (b, i, 0)),
            pl.BlockSpec((1,) + STATE_SHAPE, lambda b, i: (b, 0, 0, 0)),
            mod(0), mod(1), mod(2),
            const((1, d)), const((1, d)),
            const((d, M_COLS)),
            const((LANES, QK)), const((1, QK)),
            const((1, DV)),
            const((3, SC_W)), const((1, SC_W)),
            const((d, d)),
        ],
        out_specs=pl.BlockSpec((1, TILE_SCAN, d), lambda b, i: (b, i, 0)),
        out_shape=jax.ShapeDtypeStruct((bsz, t, d), F32),
        scratch_shapes=[pltpu.VMEM((TILE_SCAN, M_COLS), F32), pltpu.VMEM((TILE_SCAN, GLA_W), F32),
                        pltpu.VMEM(STATE_SHAPE, F32)],
        compiler_params=pltpu.CompilerParams(
            dimension_semantics=("arbitrary", "arbitrary"), vmem_limit_bytes=VMEM_LIMIT),
        name="mixer",
    )(x, o_b, kqv, a_low, s_f, mod3, mod3, mod3, g_pre, g_post, w_fwd, wg_f, bg_f, g_head, w_sc,
      b_sc, w_out)


def _ffn_kernel(xm_ref, xp_ref, xn_ref, sh_ref, sc_ref, gt_ref, gpre_ref, gpost_ref,
                wup_ref, wcf_ref, bcf_ref, wdown_ref, out_ref, hx_ref, h_ref):
    i = pl.program_id(1)
    n = xm_ref.shape[1]
    ext = n + 2 * GRID_W

    def prep(v):
        return (_rms(v, gpre_ref[...]) * (1.0 + sc_ref[0]) + sh_ref[0]).astype(BF16)

    x1 = xm_ref[0]
    hx_ref[0:GRID_W, :] = prep(xp_ref[0])
    hx_ref[GRID_W:GRID_W + n, :] = prep(x1)
    hx_ref[GRID_W + n:ext, :] = prep(xn_ref[0])

    rowid = lax.broadcasted_iota(jnp.int32, (ext, 1), 0)
    has_prev = (i > 0).astype(F32)
    has_next = (i < pl.num_programs(1) - 1).astype(F32)
    halo = jnp.where(rowid < GRID_W, has_prev, jnp.where(rowid >= GRID_W + n, has_next, 1.0))
    pos = rowid % GRID_W

    for c in range(D_FF // MXU_N):
        cs = slice(c * MXU_N, (c + 1) * MXU_N)
        gs = slice(D_FF + c * MXU_N, D_FF + (c + 1) * MXU_N)
        u = _dot(hx_ref[...], wup_ref[:, cs]) * halo
        gate = _dot(hx_ref[GRID_W:GRID_W + n, :], wup_ref[:, gs])
        u_l = jnp.where(pos == 0, 0.0, pltpu.roll(u, 1, 0))
        u_r = jnp.where(pos == GRID_W - 1, 0.0, pltpu.roll(u, ext - 1, 0))
        y = bcf_ref[:, cs]
        for dr in range(3):
            rs = slice(dr * GRID_W, dr * GRID_W + n)
            y = (y + wcf_ref[3 * dr:3 * dr + 1, cs] * u_l[rs]
                 + wcf_ref[3 * dr + 1:3 * dr + 2, cs] * u[rs]
                 + wcf_ref[3 * dr + 2:3 * dr + 3, cs] * u_r[rs])
        h_ref[:, cs] = (_silu(y) * gate).astype(BF16)

    z = _dot(h_ref[...], wdown_ref[...])
    out_ref[0] = x1 + gt_ref[0] * _rms(z, gpost_ref[...])


def _ffn(x1, mod3, g_pre, g_post, w_up, w_cf, b_cf, w_down):
    bsz, t, d = x1.shape
    nt = t // TILE_FFN
    rows_per_tile = TILE_FFN // GRID_W
    n_rows = t // GRID_W
    const = lambda shape: pl.BlockSpec(shape, lambda b, i: (0,) * len(shape),
                                       pipeline_mode=pl.Buffered(1))
    mod = lambda j: pl.BlockSpec((1, 1, d), lambda b, i: (b * 6 + j, 0, 0))
    return pl.pallas_call(
        _ffn_kernel,
        grid=(bsz, nt),
        in_specs=[
            pl.BlockSpec((1, TILE_FFN, d), lambda b, i: (b, i, 0)),
            pl.BlockSpec((1, GRID_W, d),
                         lambda b, i: (b, jnp.maximum(i * rows_per_tile - 1, 0), 0)),
            pl.BlockSpec((1, GRID_W, d),
                         lambda b, i: (b, jnp.minimum((i + 1) * rows_per_tile, n_rows - 1), 0)),
            mod(3), mod(4), mod(5),
            const((1, d)), const((1, d)),
            const((d, 2 * D_FF)),
            const((9, D_FF)), const((1, D_FF)),
            const((D_FF, d)),
        ],
        out_specs=pl.BlockSpec((1, TILE_FFN, d), lambda b, i: (b, i, 0)),
        out_shape=jax.ShapeDtypeStruct((bsz, t, d), F32),
        scratch_shapes=[pltpu.VMEM((TILE_FFN + 2 * GRID_W, d), BF16),
                        pltpu.VMEM((TILE_FFN, D_FF), BF16)],
        compiler_params=pltpu.CompilerParams(
            dimension_semantics=("arbitrary", "arbitrary"), vmem_limit_bytes=VMEM_LIMIT),
        name="ffn",
    )(x1, x1, x1, mod3, mod3, mod3, g_pre, g_post, w_up, w_cf, b_cf, w_down)


def kernel(x, c, ctx, c_ctx, w_ada, b_ada, g_pre_mix, g_post_mix, g_pre_ffn, g_post_ffn,
           w_in, w_af, b_af, w_ab, b_ab, g_head, w_sc, b_sc, w_out, w_up, w_cf, b_cf, w_down):
    bsz, t, d = x.shape
    assert d == D_MODEL and t % TILE_SCAN == 0 and t % TILE_FFN == 0 and w_in.shape[0] == 1

    cond = jnp.concatenate([c, c_ctx[None, :], jnp.zeros((16 - bsz - 1, d), F32)], axis=0)
    mod = _adaln(cond, w_ada[0], b_ada[0][None, :])
    mod3 = mod.reshape(16 * 6, 1, d)

    wi = w_in[0]
    sl = lambda a, b: wi[:, a:b]
    gate_cols = jnp.concatenate(
        [sl(COL_AF, COL_Q), jnp.zeros((d, LANES - 2 * LOWRANK), F32)], axis=1)
    w_fwd = sl(COL_OG, D_IN).astype(BF16)
    w_bwd = jnp.concatenate(
        [sl(COL_K, COL_V), sl(COL_Q, COL_OG), sl(COL_V, COL_AF), gate_cols], axis=1).astype(BF16)
    w_ctx = jnp.concatenate([sl(COL_K, COL_V), sl(COL_V, COL_AF), gate_cols], axis=1).astype(BF16)
    wg = jnp.zeros((LANES, 2 * QK), F32)
    wg = wg.at[0:LOWRANK, 0:QK].set(w_af[0]).at[LOWRANK:2 * LOWRANK, QK:].set(w_ab[0]).astype(BF16)
    bg = jnp.concatenate([b_af[0], b_ab[0]])[None, :]

    s_f, s_b = _ctx_states(ctx, mod3, g_pre_mix, w_ctx, wg, bg)
    o_b, kqv, a_low = _gla_bwd(x, s_b, mod3, g_pre_mix, w_bwd, wg[:, QK:], bg[:, QK:])
    x1 = _mixer(x, o_b, kqv, a_low, s_f, mod3, g_pre_mix, g_post_mix, w_fwd, wg[:, :QK],
                bg[:, :QK], g_head, w_sc[0], b_sc, w_out[0].astype(BF16))
    return _ffn(x1, mod3, g_pre_ffn, g_post_ffn, w_up[0].astype(BF16),
                w_cf[0].reshape(9, D_FF), b_cf, w_down[0].astype(BF16))
```

```python
import functools

import jax
import jax.numpy as jnp
from jax import lax
from jax.experimental import pallas as pl
from jax.experimental.pallas import tpu as pltpu

F32 = jnp.float32
BF16 = jnp.bfloat16

D_MODEL = 1024
GRID_W = 64
HEADS = 4
DK = 64
DV = 128
QK = HEADS * DK
GLA_W = HEADS * DV
LOWRANK = 16
TAU = 16.0
CHUNK = 64
SC_W = 512
D_FF = 2816
EPS = 1e-6

COL_K = 0
COL_V = COL_K + QK
COL_AF = COL_V + GLA_W
COL_AB = COL_AF + LOWRANK
COL_Q = COL_AB + LOWRANK
COL_OG = COL_Q + QK
COL_SB = COL_OG + GLA_W
COL_SC = COL_SB + SC_W
COL_SX = COL_SC + SC_W
D_IN = COL_SX + SC_W

LANES = 128
PAIRS = HEADS // 2
PAIR_K = 2 * DK
PAIR_V = 2 * DV
STATE_SHAPE = (PAIRS, PAIR_K, PAIR_V)
MXU_N = 256
TILE_SCAN = 1024
TILE_FFN = 1024
VMEM_LIMIT = 56 * 1024 * 1024

M_OG, M_SB, M_SC, M_SX = 0, 512, 1024, 1536
M_COLS = M_SX + SC_W
B_K, B_Q, B_V, B_A = 0, 256, 512, 1024
B_COLS = B_A + LANES
C_K, C_V, C_A = 0, 256, 768
C_COLS = C_A + LANES


def _dot(a, b):
    return jnp.dot(a, b, preferred_element_type=F32)


def _dot_tn(a, b):
    return lax.dot_general(a, b, (((0,), (0,)), ((), ())), preferred_element_type=F32)


def _dot_nt(a, b):
    return lax.dot_general(a, b, (((1,), (1,)), ((), ())), preferred_element_type=F32)


def _split_bf16(x):
    hi = x.astype(BF16)
    lo = (x - hi.astype(F32)).astype(BF16)
    return hi, lo


def _rms(x, g):
    ms = jnp.mean(x * x, axis=-1, keepdims=True)
    return x * lax.rsqrt(ms + EPS) * g


def _silu(x):
    return x * jax.nn.sigmoid(x)


def _log_sigmoid(a):
    return jnp.minimum(a, 0.0) - jnp.log1p(jnp.exp(-jnp.abs(a)))


def _adaln_kernel(c_ref, w_ref, b_ref, o_ref):
    s = _silu(c_ref[...])
    s_hi, s_lo = _split_bf16(s)
    w_hi, w_lo = _split_bf16(w_ref[...])
    o_ref[...] = _dot(s_hi, w_hi) + _dot(s_lo, w_hi) + _dot(s_hi, w_lo) + b_ref[...]


def _adaln(cond, w, b):
    rows, d = cond.shape
    n = w.shape[1]
    bn = 1024
    return pl.pallas_call(
        _adaln_kernel,
        grid=(n // bn,),
        in_specs=[
            pl.BlockSpec((rows, d), lambda j: (0, 0)),
            pl.BlockSpec((d, bn), lambda j: (0, j)),
            pl.BlockSpec((1, bn), lambda j: (0, j)),
        ],
        out_specs=pl.BlockSpec((rows, bn), lambda j: (0, j)),
        out_shape=jax.ShapeDtypeStruct((rows, n), F32),
        compiler_params=pltpu.CompilerParams(
            dimension_semantics=("arbitrary",), vmem_limit_bytes=VMEM_LIMIT),
        name="adaln",
    )(cond, w, b)


def _pair_block_mask():
    r = lax.broadcasted_iota(jnp.int32, (PAIR_K, PAIR_V), 0) // DK
    c = lax.broadcasted_iota(jnp.int32, (PAIR_K, PAIR_V), 1) // DV
    return r == c


def _pair_states(full, mask):
    return [jnp.where(mask, full[p * PAIR_K:(p + 1) * PAIR_K, p * PAIR_V:(p + 1) * PAIR_V], 0.0)
            for p in range(PAIRS)]


def _ctx_kernel(ctx_ref, sh_ref, sc_ref, gpre_ref, w_ref, wg_ref, bg_ref, sf_ref, sb_ref):
    n = ctx_ref.shape[1]
    hc = _rms(ctx_ref[0], gpre_ref[...]) * (1.0 + sc_ref[0]) + sh_ref[0]
    pc = _dot(hc.astype(BF16), w_ref[...])
    k = pc[:, C_K:C_K + QK]
    v = pc[:, C_V:C_V + GLA_W].astype(BF16)
    a = _dot(pc[:, C_A:C_A + LANES].astype(BF16), wg_ref[...]) + bg_ref[...]
    g = _log_sigmoid(a) * (1.0 / TAU)
    row = lax.broadcasted_iota(jnp.int32, (n, n), 0)
    col = lax.broadcasted_iota(jnp.int32, (n, n), 1)
    later = (col > row).astype(BF16)
    earlier = (col < row).astype(BF16)
    gf_hi, gf_lo = _split_bf16(g[:, :QK])
    gb_hi, gb_lo = _split_bf16(g[:, QK:])
    wf = jnp.exp(_dot(later, gf_hi) + _dot(later, gf_lo))
    wb = jnp.exp(_dot(earlier, gb_hi) + _dot(earlier, gb_lo))
    mask = _pair_block_mask()
    for p, s in enumerate(_pair_states(_dot_tn((k * wf).astype(BF16), v), mask)):
        sf_ref[0, p] = s
    for p, s in enumerate(_pair_states(_dot_tn((k * wb).astype(BF16), v), mask)):
        sb_ref[0, p] = s


def _ctx_states(ctx, mod3, g_pre, w_ctx, wg, bg):
    bsz, n, d = ctx.shape
    ctx_row = bsz * 6
    const = lambda shape: pl.BlockSpec(shape, lambda b: (0,) * len(shape))
    state = jax.ShapeDtypeStruct((bsz,) + STATE_SHAPE, F32)
    return pl.pallas_call(
        _ctx_kernel,
        grid=(bsz,),
        in_specs=[
            pl.BlockSpec((1, n, d), lambda b: (b, 0, 0)),
            pl.BlockSpec((1, 1, d), lambda b: (ctx_row + 0, 0, 0)),
            pl.BlockSpec((1, 1, d), lambda b: (ctx_row + 1, 0, 0)),
            const((1, d)),
            const((d, C_COLS)),
            const((LANES, 2 * QK)),
            const((1, 2 * QK)),
        ],
        out_specs=[pl.BlockSpec((1,) + STATE_SHAPE, lambda b: (b, 0, 0, 0))] * 2,
        out_shape=[state, state],
        compiler_params=pltpu.CompilerParams(
            dimension_semantics=("arbitrary",), vmem_limit_bytes=VMEM_LIMIT),
        name="ctx_state",
    )(ctx, mod3, mod3, g_pre, w_ctx, wg, bg)


def _gla_tile(p_ref, cols, g, state_ref, o_ref, reverse):
    col_k, col_q, col_v = cols
    n_chunks = p_ref.shape[0] // CHUNK
    ii = lax.broadcasted_iota(jnp.int32, (CHUNK, CHUNK), 0)
    jj = lax.broadcasted_iota(jnp.int32, (CHUNK, CHUNK), 1)
    i_all = lax.broadcasted_iota(jnp.int32, (CHUNK, HEADS * CHUNK), 0)
    j_all = lax.broadcasted_iota(jnp.int32, (CHUNK, HEADS * CHUNK), 1) % CHUNK
    if reverse:
        tri = (jj >= ii).astype(BF16)
        keep = j_all > i_all
        ref_row, last_row = CHUNK // 2, 0
    else:
        tri = (jj <= ii).astype(BF16)
        keep = j_all <= i_all
        ref_row, last_row = CHUNK // 2 - 1, CHUNK - 1
    tri2 = jnp.concatenate([tri, tri], axis=1)
    own_head = (lax.broadcasted_iota(jnp.int32, (HEADS * CHUNK, QK), 0) // CHUNK
                == lax.broadcasted_iota(jnp.int32, (HEADS * CHUNK, QK), 1) // DK)
    block = _pair_block_mask()

    chunks = range(n_chunks)
    order = range(n_chunks - 1, -1, -1) if reverse else range(n_chunks)
    rows = [slice(c * CHUNK, (c + 1) * CHUNK) for c in chunks]
    pair_k = [slice(p * PAIR_K, (p + 1) * PAIR_K) for p in range(PAIRS)]
    pair_v = [slice(p * PAIR_V, (p + 1) * PAIR_V) for p in range(PAIRS)]

    bs = []
    for c in chunks:
        g_hi, g_lo = _split_bf16(g[rows[c], :])
        bs.append(_dot(tri2, jnp.concatenate([g_hi, g_lo], axis=0)))

    q_ts, k_stacks, q_hs, k_hs, decays = [], [], [], [], []
    for c in chunks:
        b = bs[c]
        b_ref = b[ref_row:ref_row + 1, :]
        b_last = b[last_row:last_row + 1, :]
        e = b - b_ref
        q_t = p_ref[rows[c], col_q:col_q + QK] * (DK ** -0.5) * jnp.exp(e)
        k_t = p_ref[rows[c], col_k:col_k + QK] * jnp.exp(-e)
        q_ts.append(q_t.astype(BF16))
        k_stacks.append(
            jnp.where(own_head, jnp.concatenate([k_t] * HEADS, axis=0), 0.0).astype(BF16))
        q_hs.append((q_t * jnp.exp(b_ref)).astype(BF16))
        k_hs.append((k_t * jnp.exp(b_last - b_ref)).astype(BF16))
        decays.append(jnp.exp(b_last))

    atts = [jnp.where(keep, _dot_nt(q_ts[c], k_stacks[c]), 0.0).astype(BF16) for c in chunks]

    upds, decay_rows, v_diags = [], [], []
    for c in chunks:
        u_c, d_c, v_c = [], [], []
        for p in range(PAIRS):
            v_pair = p_ref[rows[c], col_v + p * PAIR_V:col_v + (p + 1) * PAIR_V].astype(F32)
            v_c.append(jnp.where(block, jnp.concatenate([v_pair, v_pair], axis=0), 0.0).astype(BF16))
            u_c.append(jnp.where(block, _dot_tn(k_hs[c][:, pair_k[p]], v_pair.astype(BF16)), 0.0))
            d_c.append(jnp.broadcast_to(decays[c][:, pair_k[p]], (PAIR_V, PAIR_K)).T)
        upds.append(u_c)
        decay_rows.append(d_c)
        v_diags.append(v_c)

    seen = [[None] * PAIRS for _ in chunks]
    for p in range(PAIRS):
        s = state_ref[p]
        for c in order:
            seen[c][p] = s.astype(BF16)
            s = s * decay_rows[c][p] + upds[c][p]
        state_ref[p] = s

    for c in chunks:
        for p in range(PAIRS):
            lhs = jnp.concatenate([atts[c][:, pair_k[p]], q_hs[c][:, pair_k[p]]], axis=1)
            rhs = jnp.concatenate([v_diags[c][p], seen[c][p]], axis=0)
            o_ref[rows[c], pair_v[p]] = _dot(lhs, rhs)


def _gate(a_low, wg_ref, bg_ref):
    a = _dot(a_low.astype(BF16), wg_ref[...]) + bg_ref[...]
    return _log_sigmoid(a) * (1.0 / TAU)


def _gla_bwd_kernel(x_ref, s0_ref, sh_ref, sc_ref, gpre_ref, w_ref, wg_ref, bg_ref,
                    ob_ref, kqv_ref, alow_ref, p_ref, state_ref):
    @pl.when(pl.program_id(1) == 0)
    def _():
        state_ref[...] = s0_ref[0]

    hx = _rms(x_ref[0], gpre_ref[...]) * (1.0 + sc_ref[0]) + sh_ref[0]
    p_ref[...] = _dot(hx.astype(BF16), w_ref[...])
    kqv_ref[0] = p_ref[:, 0:B_A].astype(BF16)
    alow_ref[0] = p_ref[:, B_A:B_A + LANES]
    g = _gate(p_ref[:, B_A:B_A + LANES], wg_ref, bg_ref)
    _gla_tile(p_ref, (B_K, B_Q, B_V), g, state_ref, ob_ref.at[0], reverse=True)


def _gla_bwd(x, s_b, mod3, g_pre, w_bwd, wg_b, bg_b):
    bsz, t, d = x.shape
    nt = t // TILE_SCAN
    const = lambda shape: pl.BlockSpec(shape, lambda b, i: (0,) * len(shape),
                                       pipeline_mode=pl.Buffered(1))
    return pl.pallas_call(
        _gla_bwd_kernel,
        grid=(bsz, nt),
        in_specs=[
            pl.BlockSpec((1, TILE_SCAN, d), lambda b, i: (b, nt - 1 - i, 0)),
            pl.BlockSpec((1,) + STATE_SHAPE, lambda b, i: (b, 0, 0, 0)),
            pl.BlockSpec((1, 1, d), lambda b, i: (b * 6 + 0, 0, 0)),
            pl.BlockSpec((1, 1, d), lambda b, i: (b * 6 + 1, 0, 0)),
            const((1, d)),
            const((d, B_COLS)),
            const((LANES, QK)),
            const((1, QK)),
        ],
        out_specs=[pl.BlockSpec((1, TILE_SCAN, GLA_W), lambda b, i: (b, nt - 1 - i, 0)),
                   pl.BlockSpec((1, TILE_SCAN, B_A), lambda b, i: (b, nt - 1 - i, 0)),
                   pl.BlockSpec((1, TILE_SCAN, LANES), lambda b, i: (b, nt - 1 - i, 0))],
        out_shape=[jax.ShapeDtypeStruct((bsz, t, GLA_W), F32),
                   jax.ShapeDtypeStruct((bsz, t, B_A), BF16),
                   jax.ShapeDtypeStruct((bsz, t, LANES), F32)],
        scratch_shapes=[pltpu.VMEM((TILE_SCAN, B_COLS), F32), pltpu.VMEM(STATE_SHAPE, F32)],
        compiler_params=pltpu.CompilerParams(
            dimension_semantics=("arbitrary", "arbitrary"), vmem_limit_bytes=VMEM_LIMIT),
        name="gla_bwd",
    )(x, s_b, mod3, mod3, g_pre, w_bwd, wg_b, bg_b)


def _mixer_kernel(x_ref, ob_ref, kqv_ref, alow_ref, s0_ref, sh_ref, sc_ref, gt_ref, gpre_ref,
                  gpost_ref, w_ref, wg_ref, bg_ref, ghead_ref, wsc_ref, bsc_ref, wout_ref,
                  x1_ref, p_ref, o_ref, state_ref):
    @pl.when(pl.program_id(1) == 0)
    def _():
        state_ref[...] = s0_ref[0]

    n = x_ref.shape[1]
    x = x_ref[0]
    hx = _rms(x, gpre_ref[...]) * (1.0 + sc_ref[0]) + sh_ref[0]
    p_ref[...] = _dot(hx.astype(BF16), w_ref[...])
    g = _gate(alow_ref[0], wg_ref, bg_ref)
    _gla_tile(kqv_ref.at[0], (B_K, B_Q, B_V), g, state_ref, o_ref, reverse=False)

    heads = []
    for h in range(HEADS):
        cols = slice(h * DV, (h + 1) * DV)
        heads.append(_rms(o_ref[:, cols] + ob_ref[0, :, cols], ghead_ref[...]))
    o_gla = jnp.concatenate(heads, axis=1) * _silu(p_ref[:, M_OG:M_OG + GLA_W])

    u = p_ref[:, M_SC:M_SC + SC_W] * p_ref[:, M_SX:M_SX + SC_W]
    pos = lax.broadcasted_iota(jnp.int32, (n, 1), 0) % GRID_W
    left = jnp.where(pos == 0, 0.0, pltpu.roll(u, 1, 0))
    right = jnp.where(pos == GRID_W - 1, 0.0, pltpu.roll(u, n - 1, 0))
    conv = wsc_ref[0:1, :] * left + wsc_ref[1:2, :] * u + wsc_ref[2:3, :] * right + bsc_ref[...]
    o_sc = p_ref[:, M_SB:M_SB + SC_W] * conv

    yx = jnp.concatenate([o_gla, o_sc], axis=1).astype(BF16)
    x1_ref[0] = x + gt_ref[0] * _rms(_dot(yx, wout_ref[...]), gpost_ref[...])


def _mixer(x, o_b, kqv, a_low, s_f, mod3, g_pre, g_post, w_fwd, wg_f, bg_f, g_head, w_sc, b_sc,
           w_out):
    bsz, t, d = x.shape
    nt = t // TILE_SCAN
    const = lambda shape: pl.BlockSpec(shape, lambda b, i: (0,) * len(shape),
                                       pipeline_mode=pl.Buffered(1))
    mod = lambda j: pl.BlockSpec((1, 1, d), lambda b, i: (b * 6 + j, 0, 0))
    return pl.pallas_call(
        _mixer_kernel,
        grid=(bsz, nt),
        in_specs=[
            pl.BlockSpec((1, TILE_SCAN, d), lambda b, i: (b, i, 0)),
            pl.BlockSpec((1, TILE_SCAN, GLA_W), lambda b, i: (b, i, 0)),
            pl.BlockSpec((1, TILE_SCAN, B_A), lambda b, i: (b, i, 0)),
            pl.BlockSpec((1, TILE_SCAN, LANES), lambda b, i: (b, i, 0)),
            pl.BlockSpec((1,) + STATE_SHAPE, lambda b, i: (b, 0, 0, 0)),
            mod(0), mod(1), mod(2),
            const((1, d)), const((1, d)),
            const((d, M_COLS)),
            const((LANES, QK)), const((1, QK)),
            const((1, DV)),
            const((3, SC_W)), const((1, SC_W)),
            const((d, d)),
        ],
        out_specs=pl.BlockSpec((1, TILE_SCAN, d), lambda b, i: (b, i, 0)),
        out_shape=jax.ShapeDtypeStruct((bsz, t, d), F32),
        scratch_shapes=[pltpu.VMEM((TILE_SCAN, M_COLS), F32), pltpu.VMEM((TILE_SCAN, GLA_W), F32),
                        pltpu.VMEM(STATE_SHAPE, F32)],
        compiler_params=pltpu.CompilerParams(
            dimension_semantics=("arbitrary", "arbitrary"), vmem_limit_bytes=VMEM_LIMIT),
        name="mixer",
    )(x, o_b, kqv, a_low, s_f, mod3, mod3, mod3, g_pre, g_post, w_fwd, wg_f, bg_f, g_head, w_sc,
      b_sc, w_out)


def _ffn_kernel(xm_ref, xp_ref, xn_ref, sh_ref, sc_ref, gt_ref, gpre_ref, gpost_ref,
                wup_ref, wcf_ref, bcf_ref, wdown_ref, out_ref, hx_ref, h_ref):
    i = pl.program_id(1)
    n = xm_ref.shape[1]
    ext = n + 2 * GRID_W

    def prep(v):
        return (_rms(v, gpre_ref[...]) * (1.0 + sc_ref[0]) + sh_ref[0]).astype(BF16)

    x1 = xm_ref[0]
    hx_ref[0:GRID_W, :] = prep(xp_ref[0])
    hx_ref[GRID_W:GRID_W + n, :] = prep(x1)
    hx_ref[GRID_W + n:ext, :] = prep(xn_ref[0])

    rowid = lax.broadcasted_iota(jnp.int32, (ext, 1), 0)
    has_prev = (i > 0).astype(F32)
    has_next = (i < pl.num_programs(1) - 1).astype(F32)
    pos = rowid % GRID_W

    for c in range(D_FF // MXU_N):
        cs = slice(c * MXU_N, (c + 1) * MXU_N)
        gs = slice(D_FF + c * MXU_N, D_FF + (c + 1) * MXU_N)
        u = _dot(hx_ref[...], wup_ref[:, cs])
        u = jnp.concatenate([u[0:GRID_W] * has_prev, u[GRID_W:GRID_W + n],
                             u[GRID_W + n:ext] * has_next], axis=0)
        gate = _dot(hx_ref[GRID_W:GRID_W + n, :], wup_ref[:, gs])
        u_l = jnp.where(pos == 0, 0.0, pltpu.roll(u, 1, 0))
        u_r = jnp.where(pos == GRID_W - 1, 0.0, pltpu.roll(u, ext - 1, 0))
        y = bcf_ref[:, cs]
        for dr in range(3):
            rs = slice(dr * GRID_W, dr * GRID_W + n)
            y = (y + wcf_ref[3 * dr:3 * dr + 1, cs] * u_l[rs]
                 + wcf_ref[3 * dr + 1:3 * dr + 2, cs] * u[rs]
                 + wcf_ref[3 * dr + 2:3 * dr + 3, cs] * u_r[rs])
        h_ref[:, cs] = (_silu(y) * gate).astype(BF16)

    z = _dot(h_ref[...], wdown_ref[...])
    out_ref[0] = x1 + gt_ref[0] * _rms(z, gpost_ref[...])


def _ffn(x1, mod3, g_pre, g_post, w_up, w_cf, b_cf, w_down):
    bsz, t, d = x1.shape
    nt = t // TILE_FFN
    rows_per_tile = TILE_FFN // GRID_W
    n_rows = t // GRID_W
    const = lambda shape: pl.BlockSpec(shape, lambda b, i: (0,) * len(shape),
                                       pipeline_mode=pl.Buffered(1))
    mod = lambda j: pl.BlockSpec((1, 1, d), lambda b, i: (b * 6 + j, 0, 0))
    return pl.pallas_call(
        _ffn_kernel,
        grid=(bsz, nt),
        in_specs=[
            pl.BlockSpec((1, TILE_FFN, d), lambda b, i: (b, i, 0)),
            pl.BlockSpec((1, GRID_W, d),
                         lambda b, i: (b, jnp.maximum(i * rows_per_tile - 1, 0), 0)),
            pl.BlockSpec((1, GRID_W, d),
                         lambda b, i: (b, jnp.minimum((i + 1) * rows_per_tile, n_rows - 1), 0)),
            mod(3), mod(4), mod(5),
            const((1, d)), const((1, d)),
            const((d, 2 * D_FF)),
            const((9, D_FF)), const((1, D_FF)),
            const((D_FF, d)),
        ],
        out_specs=pl.BlockSpec((1, TILE_FFN, d), lambda b, i: (b, i, 0)),
        out_shape=jax.ShapeDtypeStruct((bsz, t, d), F32),
        scratch_shapes=[pltpu.VMEM((TILE_FFN + 2 * GRID_W, d), BF16),
                        pltpu.VMEM((TILE_FFN, D_FF), BF16)],
        compiler_params=pltpu.CompilerParams(
            dimension_semantics=("arbitrary", "arbitrary"), vmem_limit_bytes=VMEM_LIMIT),
        name="ffn",
    )(x1, x1, x1, mod3, mod3, mod3, g_pre, g_post, w_up, w_cf, b_cf, w_down)


def kernel(x, c, ctx, c_ctx, w_ada, b_ada, g_pre_mix, g_post_mix, g_pre_ffn, g_post_ffn,
           w_in, w_af, b_af, w_ab, b_ab, g_head, w_sc, b_sc, w_out, w_up, w_cf, b_cf, w_down):
    bsz, t, d = x.shape
    assert d == D_MODEL and t % TILE_SCAN == 0 and t % TILE_FFN == 0 and w_in.shape[0] == 1

    cond = jnp.concatenate([c, c_ctx[None, :], jnp.zeros((16 - bsz - 1, d), F32)], axis=0)
    mod = _adaln(cond, w_ada[0], b_ada[0][None, :])
    mod3 = mod.reshape(16 * 6, 1, d)

    wi = w_in[0]
    sl = lambda a, b: wi[:, a:b]
    gate_cols = jnp.concatenate(
        [sl(COL_AF, COL_Q), jnp.zeros((d, LANES - 2 * LOWRANK), F32)], axis=1)
    w_fwd = sl(COL_OG, D_IN).astype(BF16)
    w_bwd = jnp.concatenate(
        [sl(COL_K, COL_V), sl(COL_Q, COL_OG), sl(COL_V, COL_AF), gate_cols], axis=1).astype(BF16)
    w_ctx = jnp.concatenate([sl(COL_K, COL_V), sl(COL_V, COL_AF), gate_cols], axis=1).astype(BF16)
    wg = jnp.zeros((LANES, 2 * QK), F32)
    wg = wg.at[0:LOWRANK, 0:QK].set(w_af[0]).at[LOWRANK:2 * LOWRANK, QK:].set(w_ab[0]).astype(BF16)
    bg = jnp.concatenate([b_af[0], b_ab[0]])[None, :]

    s_f, s_b = _ctx_states(ctx, mod3, g_pre_mix, w_ctx, wg, bg)
    o_b, kqv, a_low = _gla_bwd(x, s_b, mod3, g_pre_mix, w_bwd, wg[:, QK:], bg[:, QK:])
    x1 = _mixer(x, o_b, kqv, a_low, s_f, mod3, g_pre_mix, g_post_mix, w_fwd, wg[:, :QK],
                bg[:, :QK], g_head, w_sc[0], b_sc, w_out[0].astype(BF16))
    return _ffn(x1, mod3, g_pre_ffn, g_post_ffn, w_up[0].astype(BF16),
                w_cf[0].reshape(9, D_FF), b_cf, w_down[0].astype(BF16))
```

```python
import functools

import jax
import jax.numpy as jnp
from jax import lax
from jax.experimental import pallas as pl
from jax.experimental.pallas import tpu as pltpu

F32 = jnp.float32
BF16 = jnp.bfloat16

D_MODEL = 1024
GRID_W = 64
HEADS = 4
DK = 64
DV = 128
QK = HEADS * DK
GLA_W = HEADS * DV
LOWRANK = 16
TAU = 16.0
CHUNK = 64
SC_W = 512
D_FF = 2816
EPS = 1e-6

COL_K = 0
COL_V = COL_K + QK
COL_AF = COL_V + GLA_W
COL_AB = COL_AF + LOWRANK
COL_Q = COL_AB + LOWRANK
COL_OG = COL_Q + QK
COL_SB = COL_OG + GLA_W
COL_SC = COL_SB + SC_W
COL_SX = COL_SC + SC_W
D_IN = COL_SX + SC_W

LANES = 128
PAIRS = HEADS // 2
PAIR_K = 2 * DK
PAIR_V = 2 * DV
STATE_SHAPE = (PAIRS, PAIR_K, PAIR_V)
MXU_N = 256
FFN_DOWN_SPLIT = 8 * MXU_N
TILE_SCAN = 1024
TILE_FFN = 1024
VMEM_LIMIT = 56 * 1024 * 1024

M_OG, M_SB, M_SC, M_SX = 0, 512, 1024, 1536
M_COLS = M_SX + SC_W
B_K, B_Q, B_V, B_A = 0, 256, 512, 1024
B_COLS = B_A + LANES
C_K, C_V, C_A = 0, 256, 768
C_COLS = C_A + LANES


def _dot(a, b):
    return jnp.dot(a, b, preferred_element_type=F32)


def _dot_tn(a, b):
    return lax.dot_general(a, b, (((0,), (0,)), ((), ())), preferred_element_type=F32)


def _dot_nt(a, b):
    return lax.dot_general(a, b, (((1,), (1,)), ((), ())), preferred_element_type=F32)


def _split_bf16(x):
    hi = x.astype(BF16)
    lo = (x - hi.astype(F32)).astype(BF16)
    return hi, lo


def _rms(x, g):
    ms = jnp.mean(x * x, axis=-1, keepdims=True)
    return x * lax.rsqrt(ms + EPS) * g


def _silu(x):
    return x * jax.nn.sigmoid(x)


def _log_sigmoid(a):
    return jnp.minimum(a, 0.0) - jnp.log1p(jnp.exp(-jnp.abs(a)))


def _adaln_kernel(c_ref, w_ref, b_ref, o_ref):
    s = _silu(c_ref[...])
    s_hi, s_lo = _split_bf16(s)
    w_hi, w_lo = _split_bf16(w_ref[...])
    o_ref[...] = _dot(s_hi, w_hi) + _dot(s_lo, w_hi) + _dot(s_hi, w_lo) + b_ref[...]


def _adaln(cond, w, b):
    rows, d = cond.shape
    n = w.shape[1]
    bn = 1024
    return pl.pallas_call(
        _adaln_kernel,
        grid=(n // bn,),
        in_specs=[
            pl.BlockSpec((rows, d), lambda j: (0, 0)),
            pl.BlockSpec((d, bn), lambda j: (0, j)),
            pl.BlockSpec((1, bn), lambda j: (0, j)),
        ],
        out_specs=pl.BlockSpec((rows, bn), lambda j: (0, j)),
        out_shape=jax.ShapeDtypeStruct((rows, n), F32),
        compiler_params=pltpu.CompilerParams(
            dimension_semantics=("arbitrary",), vmem_limit_bytes=VMEM_LIMIT),
        name="adaln",
    )(cond, w, b)


def _pair_block_mask():
    r = lax.broadcasted_iota(jnp.int32, (PAIR_K, PAIR_V), 0) // DK
    c = lax.broadcasted_iota(jnp.int32, (PAIR_K, PAIR_V), 1) // DV
    return r == c


def _pair_states(full, mask):
    return [jnp.where(mask, full[p * PAIR_K:(p + 1) * PAIR_K, p * PAIR_V:(p + 1) * PAIR_V], 0.0)
            for p in range(PAIRS)]


def _ctx_kernel(ctx_ref, sh_ref, sc_ref, gpre_ref, w_ref, wg_ref, bg_ref, sf_ref, sb_ref):
    n = ctx_ref.shape[1]
    hc = _rms(ctx_ref[0], gpre_ref[...]) * (1.0 + sc_ref[0]) + sh_ref[0]
    pc = _dot(hc.astype(BF16), w_ref[...])
    k = pc[:, C_K:C_K + QK]
    v = pc[:, C_V:C_V + GLA_W].astype(BF16)
    a = _dot(pc[:, C_A:C_A + LANES].astype(BF16), wg_ref[...]) + bg_ref[...]
    g = _log_sigmoid(a) * (1.0 / TAU)
    row = lax.broadcasted_iota(jnp.int32, (n, n), 0)
    col = lax.broadcasted_iota(jnp.int32, (n, n), 1)
    later = (col > row).astype(BF16)
    earlier = (col < row).astype(BF16)
    gf_hi, gf_lo = _split_bf16(g[:, :QK])
    gb_hi, gb_lo = _split_bf16(g[:, QK:])
    wf = jnp.exp(_dot(later, gf_hi) + _dot(later, gf_lo))
    wb = jnp.exp(_dot(earlier, gb_hi) + _dot(earlier, gb_lo))
    mask = _pair_block_mask()
    for p, s in enumerate(_pair_states(_dot_tn((k * wf).astype(BF16), v), mask)):
        sf_ref[0, p] = s
    for p, s in enumerate(_pair_states(_dot_tn((k * wb).astype(BF16), v), mask)):
        sb_ref[0, p] = s


def _ctx_states(ctx, mod3, g_pre, w_ctx, wg, bg):
    bsz, n, d = ctx.shape
    ctx_row = bsz * 6
    const = lambda shape: pl.BlockSpec(shape, lambda b: (0,) * len(shape))
    state = jax.ShapeDtypeStruct((bsz,) + STATE_SHAPE, F32)
    return pl.pallas_call(
        _ctx_kernel,
        grid=(bsz,),
        in_specs=[
            pl.BlockSpec((1, n, d), lambda b: (b, 0, 0)),
            pl.BlockSpec((1, 1, d), lambda b: (ctx_row + 0, 0, 0)),
            pl.BlockSpec((1, 1, d), lambda b: (ctx_row + 1, 0, 0)),
            const((1, d)),
            const((d, C_COLS)),
            const((LANES, 2 * QK)),
            const((1, 2 * QK)),
        ],
        out_specs=[pl.BlockSpec((1,) + STATE_SHAPE, lambda b: (b, 0, 0, 0))] * 2,
        out_shape=[state, state],
        compiler_params=pltpu.CompilerParams(
            dimension_semantics=("arbitrary",), vmem_limit_bytes=VMEM_LIMIT),
        name="ctx_state",
    )(ctx, mod3, mod3, g_pre, w_ctx, wg, bg)


def _gla_tile(p_ref, cols, g, state_ref, o_ref, reverse):
    col_k, col_q, col_v = cols
    n_chunks = p_ref.shape[0] // CHUNK
    ii = lax.broadcasted_iota(jnp.int32, (CHUNK, CHUNK), 0)
    jj = lax.broadcasted_iota(jnp.int32, (CHUNK, CHUNK), 1)
    i_all = lax.broadcasted_iota(jnp.int32, (CHUNK, HEADS * CHUNK), 0)
    j_all = lax.broadcasted_iota(jnp.int32, (CHUNK, HEADS * CHUNK), 1) % CHUNK
    if reverse:
        tri = (jj >= ii).astype(BF16)
        keep = j_all > i_all
        ref_row, last_row = CHUNK // 2, 0
    else:
        tri = (jj <= ii).astype(BF16)
        keep = j_all <= i_all
        ref_row, last_row = CHUNK // 2 - 1, CHUNK - 1
    tri2 = jnp.concatenate([tri, tri], axis=1)
    own_head = (lax.broadcasted_iota(jnp.int32, (HEADS * CHUNK, QK), 0) // CHUNK
                == lax.broadcasted_iota(jnp.int32, (HEADS * CHUNK, QK), 1) // DK)
    block = _pair_block_mask()

    chunks = range(n_chunks)
    order = range(n_chunks - 1, -1, -1) if reverse else range(n_chunks)
    rows = [slice(c * CHUNK, (c + 1) * CHUNK) for c in chunks]
    pair_k = [slice(p * PAIR_K, (p + 1) * PAIR_K) for p in range(PAIRS)]
    pair_v = [slice(p * PAIR_V, (p + 1) * PAIR_V) for p in range(PAIRS)]

    bs = []
    for c in chunks:
        g_hi, g_lo = _split_bf16(g[rows[c], :])
        bs.append(_dot(tri2, jnp.concatenate([g_hi, g_lo], axis=0)))

    q_ts, k_stacks, q_hs, k_hs, decays = [], [], [], [], []
    for c in chunks:
        b = bs[c]
        b_ref = b[ref_row:ref_row + 1, :]
        b_last = b[last_row:last_row + 1, :]
        e = b - b_ref
        q_t = p_ref[rows[c], col_q:col_q + QK] * (DK ** -0.5) * jnp.exp(e)
        k_t = p_ref[rows[c], col_k:col_k + QK] * jnp.exp(-e)
        q_ts.append(q_t.astype(BF16))
        k_stacks.append(
            jnp.where(own_head, jnp.concatenate([k_t] * HEADS, axis=0), 0.0).astype(BF16))
        q_hs.append((q_t * jnp.exp(b_ref)).astype(BF16))
        k_hs.append((k_t * jnp.exp(b_last - b_ref)).astype(BF16))
        decays.append(jnp.exp(b_last))

    atts = [jnp.where(keep, _dot_nt(q_ts[c], k_stacks[c]), 0.0).astype(BF16) for c in chunks]

    upds, decay_rows, v_diags = [], [], []
    for c in chunks:
        u_c, d_c, v_c = [], [], []
        for p in range(PAIRS):
            v_pair = p_ref[rows[c], col_v + p * PAIR_V:col_v + (p + 1) * PAIR_V].astype(F32)
            v_c.append(jnp.where(block, jnp.concatenate([v_pair, v_pair], axis=0), 0.0).astype(BF16))
            u_c.append(jnp.where(block, _dot_tn(k_hs[c][:, pair_k[p]], v_pair.astype(BF16)), 0.0))
            d_c.append(jnp.broadcast_to(decays[c][:, pair_k[p]], (PAIR_V, PAIR_K)).T)
        upds.append(u_c)
        decay_rows.append(d_c)
        v_diags.append(v_c)

    seen = [[None] * PAIRS for _ in chunks]
    for p in range(PAIRS):
        s = state_ref[p]
        for c in order:
            seen[c][p] = s.astype(BF16)
            s = s * decay_rows[c][p] + upds[c][p]
        state_ref[p] = s

    for c in chunks:
        for p in range(PAIRS):
            lhs = jnp.concatenate([atts[c][:, pair_k[p]], q_hs[c][:, pair_k[p]]], axis=1)
            rhs = jnp.concatenate([v_diags[c][p], seen[c][p]], axis=0)
            o_ref[rows[c], pair_v[p]] = _dot(lhs, rhs)


def _gate(a_low, wg_ref, bg_ref):
    a = _dot(a_low.astype(BF16), wg_ref[...]) + bg_ref[...]
    return _log_sigmoid(a) * (1.0 / TAU)


def _gla_bwd_kernel(x_ref, s0_ref, sh_ref, sc_ref, gpre_ref, w_ref, wg_ref, bg_ref,
                    ob_ref, kqv_ref, alow_ref, p_ref, state_ref):
    @pl.when(pl.program_id(1) == 0)
    def _():
        state_ref[...] = s0_ref[0]

    hx = _rms(x_ref[0], gpre_ref[...]) * (1.0 + sc_ref[0]) + sh_ref[0]
    p_ref[...] = _dot(hx.astype(BF16), w_ref[...])
    kqv_ref[0] = p_ref[:, 0:B_A].astype(BF16)
    alow_ref[0] = p_ref[:, B_A:B_A + LANES]
    g = _gate(p_ref[:, B_A:B_A + LANES], wg_ref, bg_ref)
    _gla_tile(p_ref, (B_K, B_Q, B_V), g, state_ref, ob_ref.at[0], reverse=True)


def _gla_bwd(x, s_b, mod3, g_pre, w_bwd, wg_b, bg_b):
    bsz, t, d = x.shape
    nt = t // TILE_SCAN
    const = lambda shape: pl.BlockSpec(shape, lambda b, i: (0,) * len(shape),
                                       pipeline_mode=pl.Buffered(1))
    return pl.pallas_call(
        _gla_bwd_kernel,
        grid=(bsz, nt),
        in_specs=[
            pl.BlockSpec((1, TILE_SCAN, d), lambda b, i: (b, nt - 1 - i, 0)),
            pl.BlockSpec((1,) + STATE_SHAPE, lambda b, i: (b, 0, 0, 0)),
            pl.BlockSpec((1, 1, d), lambda b, i: (b * 6 + 0, 0, 0)),
            pl.BlockSpec((1, 1, d), lambda b, i: (b * 6 + 1, 0, 0)),
            const((1, d)),
            const((d, B_COLS)),
            const((LANES, QK)),
            const((1, QK)),
        ],
        out_specs=[pl.BlockSpec((1, TILE_SCAN, GLA_W), lambda b, i: (b, nt - 1 - i, 0)),
                   pl.BlockSpec((1, TILE_SCAN, B_A), lambda b, i: (b, nt - 1 - i, 0)),
                   pl.BlockSpec((1, TILE_SCAN, LANES), lambda b, i: (b, nt - 1 - i, 0))],
        out_shape=[jax.ShapeDtypeStruct((bsz, t, GLA_W), F32),
                   jax.ShapeDtypeStruct((bsz, t, B_A), BF16),
                   jax.ShapeDtypeStruct((bsz, t, LANES), F32)],
        scratch_shapes=[pltpu.VMEM((TILE_SCAN, B_COLS), F32), pltpu.VMEM(STATE_SHAPE, F32)],
        compiler_params=pltpu.CompilerParams(
            dimension_semantics=("arbitrary", "arbitrary"), vmem_limit_bytes=VMEM_LIMIT),
        name="gla_bwd",
    )(x, s_b, mod3, mod3, g_pre, w_bwd, wg_b, bg_b)


def _mixer_kernel(x_ref, ob_ref, kqv_ref, alow_ref, s0_ref, sh_ref, sc_ref, gt_ref, gpre_ref,
                  gpost_ref, w_ref, wg_ref, bg_ref, ghead_ref, wsc_ref, bsc_ref, wout_ref,
                  x1_ref, p_ref, o_ref, state_ref):
    @pl.when(pl.program_id(1) == 0)
    def _():
        state_ref[...] = s0_ref[0]

    n = x_ref.shape[1]
    x = x_ref[0]
    hx = _rms(x, gpre_ref[...]) * (1.0 + sc_ref[0]) + sh_ref[0]
    p_ref[...] = _dot(hx.astype(BF16), w_ref[...])
    g = _gate(alow_ref[0], wg_ref, bg_ref)
    _gla_tile(kqv_ref.at[0], (B_K, B_Q, B_V), g, state_ref, o_ref, reverse=False)

    heads = []
    for h in range(HEADS):
        cols = slice(h * DV, (h + 1) * DV)
        heads.append(_rms(o_ref[:, cols] + ob_ref[0, :, cols], ghead_ref[...]))
    o_gla = jnp.concatenate(heads, axis=1) * _silu(p_ref[:, M_OG:M_OG + GLA_W])

    u = p_ref[:, M_SC:M_SC + SC_W] * p_ref[:, M_SX:M_SX + SC_W]
    pos = lax.broadcasted_iota(jnp.int32, (n, 1), 0) % GRID_W
    left = jnp.where(pos == 0, 0.0, pltpu.roll(u, 1, 0))
    right = jnp.where(pos == GRID_W - 1, 0.0, pltpu.roll(u, n - 1, 0))
    conv = wsc_ref[0:1, :] * left + wsc_ref[1:2, :] * u + wsc_ref[2:3, :] * right + bsc_ref[...]
    o_sc = p_ref[:, M_SB:M_SB + SC_W] * conv

    yx = jnp.concatenate([o_gla, o_sc], axis=1).astype(BF16)
    x1_ref[0] = x + gt_ref[0] * _rms(_dot(yx, wout_ref[...]), gpost_ref[...])


def _mixer(x, o_b, kqv, a_low, s_f, mod3, g_pre, g_post, w_fwd, wg_f, bg_f, g_head, w_sc, b_sc,
           w_out):
    bsz, t, d = x.shape
    nt = t // TILE_SCAN
    const = lambda shape: pl.BlockSpec(shape, lambda b, i: (0,) * len(shape),
                                       pipeline_mode=pl.Buffered(1))
    mod = lambda j: pl.BlockSpec((1, 1, d), lambda b, i: (b * 6 + j, 0, 0))
    return pl.pallas_call(
        _mixer_kernel,
        grid=(bsz, nt),
        in_specs=[
            pl.BlockSpec((1, TILE_SCAN, d), lambda b, i: (b, i, 0)),
            pl.BlockSpec((1, TILE_SCAN, GLA_W), lambda b, i: (b, i, 0)),
            pl.BlockSpec((1, TILE_SCAN, B_A), lambda b, i: (b, i, 0)),
            pl.BlockSpec((1, TILE_SCAN, LANES), lambda b, i: (b, i, 0)),
            pl.BlockSpec((1,) + STATE_SHAPE, lambda b, i: (b, 0, 0, 0)),
            mod(0), mod(1), mod(2),
            const((1, d)), const((1, d)),
            const((d, M_COLS)),
            const((LANES, QK)), const((1, QK)),
            const((1, DV)),
            const((3, SC_W)), const((1, SC_W)),
            const((d, d)),
        ],
        out_specs=pl.BlockSpec((1, TILE_SCAN, d), lambda b, i: (b, i, 0)),
        out_shape=jax.ShapeDtypeStruct((bsz, t, d), F32),
        scratch_shapes=[pltpu.VMEM((TILE_SCAN, M_COLS), F32), pltpu.VMEM((TILE_SCAN, GLA_W), F32),
                        pltpu.VMEM(STATE_SHAPE, F32)],
        compiler_params=pltpu.CompilerParams(
            dimension_semantics=("arbitrary", "arbitrary"), vmem_limit_bytes=VMEM_LIMIT),
        name="mixer",
    )(x, o_b, kqv, a_low, s_f, mod3, mod3, mod3, g_pre, g_post, w_fwd, wg_f, bg_f, g_head, w_sc,
      b_sc, w_out)


def _ffn_kernel(xm_ref, xp_ref, xn_ref, sh_ref, sc_ref, gt_ref, gpre_ref, gpost_ref,
                wup_ref, wcf_ref, bcf_ref, wdown_ref, out_ref, hx_ref, h_ref):
    i = pl.program_id(1)
    n = xm_ref.shape[1]
    ext = n + 2 * GRID_W

    def prep(v):
        return (_rms(v, gpre_ref[...]) * (1.0 + sc_ref[0]) + sh_ref[0]).astype(BF16)

    x1 = xm_ref[0]
    hx_ref[0:GRID_W, :] = prep(xp_ref[0])
    hx_ref[GRID_W:GRID_W + n, :] = prep(x1)
    hx_ref[GRID_W + n:ext, :] = prep(xn_ref[0])

    rowid = lax.broadcasted_iota(jnp.int32, (ext, 1), 0)
    has_prev = (i > 0).astype(F32)
    has_next = (i < pl.num_programs(1) - 1).astype(F32)
    pos = rowid % GRID_W

    for c in range(D_FF // MXU_N):
        cs = slice(c * MXU_N, (c + 1) * MXU_N)
        gs = slice(D_FF + c * MXU_N, D_FF + (c + 1) * MXU_N)
        u = _dot(hx_ref[...], wup_ref[:, cs])
        u = jnp.concatenate([u[0:GRID_W] * has_prev, u[GRID_W:GRID_W + n],
                             u[GRID_W + n:ext] * has_next], axis=0)
        gate = _dot(hx_ref[GRID_W:GRID_W + n, :], wup_ref[:, gs])
        u_l = jnp.where(pos == 0, 0.0, pltpu.roll(u, 1, 0)).astype(BF16)
        u_r = jnp.where(pos == GRID_W - 1, 0.0, pltpu.roll(u, ext - 1, 0)).astype(BF16)
        u_c = u.astype(BF16)
        w9 = wcf_ref[:, cs].astype(BF16)
        y = bcf_ref[:, cs].astype(BF16)
        for dr in range(3):
            rs = slice(dr * GRID_W, dr * GRID_W + n)
            y = (y + w9[3 * dr:3 * dr + 1] * u_l[rs]
                 + w9[3 * dr + 1:3 * dr + 2] * u_c[rs]
                 + w9[3 * dr + 2:3 * dr + 3] * u_r[rs])
        h_ref[:, cs] = (_silu(y.astype(F32)) * gate).astype(BF16)

    z = (_dot(h_ref[:, :FFN_DOWN_SPLIT], wdown_ref[:FFN_DOWN_SPLIT, :])
         + _dot(h_ref[:, FFN_DOWN_SPLIT:], wdown_ref[FFN_DOWN_SPLIT:, :]))
    out_ref[0] = x1 + gt_ref[0] * _rms(z, gpost_ref[...])


def _ffn(x1, mod3, g_pre, g_post, w_up, w_cf, b_cf, w_down):
    bsz, t, d = x1.shape
    nt = t // TILE_FFN
    rows_per_tile = TILE_FFN // GRID_W
    n_rows = t // GRID_W
    const = lambda shape: pl.BlockSpec(shape, lambda b, i: (0,) * len(shape),
                                       pipeline_mode=pl.Buffered(1))
    mod = lambda j: pl.BlockSpec((1, 1, d), lambda b, i: (b * 6 + j, 0, 0))
    return pl.pallas_call(
        _ffn_kernel,
        grid=(bsz, nt),
        in_specs=[
            pl.BlockSpec((1, TILE_FFN, d), lambda b, i: (b, i, 0)),
            pl.BlockSpec((1, GRID_W, d),
                         lambda b, i: (b, jnp.maximum(i * rows_per_tile - 1, 0), 0)),
            pl.BlockSpec((1, GRID_W, d),
                         lambda b, i: (b, jnp.minimum((i + 1) * rows_per_tile, n_rows - 1), 0)),
            mod(3), mod(4), mod(5),
            const((1, d)), const((1, d)),
            const((d, 2 * D_FF)),
            const((9, D_FF)), const((1, D_FF)),
            const((D_FF, d)),
        ],
        out_specs=pl.BlockSpec((1, TILE_FFN, d), lambda b, i: (b, i, 0)),
        out_shape=jax.ShapeDtypeStruct((bsz, t, d), F32),
        scratch_shapes=[pltpu.VMEM((TILE_FFN + 2 * GRID_W, d), BF16),
                        pltpu.VMEM((TILE_FFN, D_FF), BF16)],
        compiler_params=pltpu.CompilerParams(
            dimension_semantics=("arbitrary", "arbitrary"), vmem_limit_bytes=VMEM_LIMIT),
        name="ffn",
    )(x1, x1, x1, mod3, mod3, mod3, g_pre, g_post, w_up, w_cf, b_cf, w_down)


def kernel(x, c, ctx, c_ctx, w_ada, b_ada, g_pre_mix, g_post_mix, g_pre_ffn, g_post_ffn,
           w_in, w_af, b_af, w_ab, b_ab, g_head, w_sc, b_sc, w_out, w_up, w_cf, b_cf, w_down):
    bsz, t, d = x.shape
    assert d == D_MODEL and t % TILE_SCAN == 0 and t % TILE_FFN == 0 and w_in.shape[0] == 1

    cond = jnp.concatenate([c, c_ctx[None, :], jnp.zeros((16 - bsz - 1, d), F32)], axis=0)
    mod = _adaln(cond, w_ada[0], b_ada[0][None, :])
    mod3 = mod.reshape(16 * 6, 1, d)

    wi = w_in[0]
    sl = lambda a, b: wi[:, a:b]
    gate_cols = jnp.concatenate(
        [sl(COL_AF, COL_Q), jnp.zeros((d, LANES - 2 * LOWRANK), F32)], axis=1)
    w_fwd = sl(COL_OG, D_IN).astype(BF16)
    w_bwd = jnp.concatenate(
        [sl(COL_K, COL_V), sl(COL_Q, COL_OG), sl(COL_V, COL_AF), gate_cols], axis=1).astype(BF16)
    w_ctx = jnp.concatenate([sl(COL_K, COL_V), sl(COL_V, COL_AF), gate_cols], axis=1).astype(BF16)
    wg = jnp.zeros((LANES, 2 * QK), F32)
    wg = wg.at[0:LOWRANK, 0:QK].set(w_af[0]).at[LOWRANK:2 * LOWRANK, QK:].set(w_ab[0]).astype(BF16)
    bg = jnp.concatenate([b_af[0], b_ab[0]])[None, :]

    s_f, s_b = _ctx_states(ctx, mod3, g_pre_mix, w_ctx, wg, bg)
    o_b, kqv, a_low = _gla_bwd(x, s_b, mod3, g_pre_mix, w_bwd, wg[:, QK:], bg[:, QK:])
    x1 = _mixer(x, o_b, kqv, a_low, s_f, mod3, g_pre_mix, g_post_mix, w_fwd, wg[:, :QK],
                bg[:, :QK], g_head, w_sc[0], b_sc, w_out[0].astype(BF16))
    return _ffn(x1, mod3, g_pre_ffn, g_post_ffn, w_up[0].astype(BF16),
                w_cf[0].reshape(9, D_FF), b_cf, w_down[0].astype(BF16))
```

```python
import functools

import jax
import jax.numpy as jnp
from jax import lax
from jax.experimental import pallas as pl
from jax.experimental.pallas import tpu as pltpu

F32 = jnp.float32
BF16 = jnp.bfloat16

D_MODEL = 1024
GRID_W = 64
HEADS = 4
DK = 64
DV = 128
QK = HEADS * DK
GLA_W = HEADS * DV
LOWRANK = 16
TAU = 16.0
CHUNK = 64
SC_W = 512
D_FF = 2816
EPS = 1e-6

COL_K = 0
COL_V = COL_K + QK
COL_AF = COL_V + GLA_W
COL_AB = COL_AF + LOWRANK
COL_Q = COL_AB + LOWRANK
COL_OG = COL_Q + QK
COL_SB = COL_OG + GLA_W
COL_SC = COL_SB + SC_W
COL_SX = COL_SC + SC_W
D_IN = COL_SX + SC_W

LANES = 128
PAIRS = HEADS // 2
PAIR_K = 2 * DK
PAIR_V = 2 * DV
STATE_SHAPE = (PAIRS, PAIR_K, PAIR_V)
MXU_N = 256
FFN_DOWN_SPLIT = 8 * MXU_N
TILE_SCAN = 1024
TILE_FFN = 1024
VMEM_LIMIT = 56 * 1024 * 1024

M_OG, M_SB, M_SC, M_SX = 0, 512, 1024, 1536
M_COLS = M_SX + SC_W
GATE_PAD = LANES
B_K, B_Q, B_V, B_A = 0, 256, 512, 1024
B_COLS = B_A + GATE_PAD
C_K, C_V, C_A = 0, 256, 768
C_COLS = C_A + GATE_PAD


def _dot(a, b):
    return jnp.dot(a, b, preferred_element_type=F32)


def _dot_tn(a, b):
    return lax.dot_general(a, b, (((0,), (0,)), ((), ())), preferred_element_type=F32)


def _dot_nt(a, b):
    return lax.dot_general(a, b, (((1,), (1,)), ((), ())), preferred_element_type=F32)


def _split_bf16(x):
    hi = x.astype(BF16)
    lo = (x - hi.astype(F32)).astype(BF16)
    return hi, lo


def _rms(x, g):
    ms = jnp.mean(x * x, axis=-1, keepdims=True)
    return x * lax.rsqrt(ms + EPS) * g


def _silu(x):
    return x * jax.nn.sigmoid(x)


def _log_sigmoid(a):
    return jnp.minimum(a, 0.0) - jnp.log1p(jnp.exp(-jnp.abs(a)))


def _adaln_kernel(c_ref, w_ref, b_ref, o_ref):
    s = _silu(c_ref[...])
    s_hi, s_lo = _split_bf16(s)
    w_hi, w_lo = _split_bf16(w_ref[...])
    o_ref[...] = _dot(s_hi, w_hi) + _dot(s_lo, w_hi) + _dot(s_hi, w_lo) + b_ref[...]


def _adaln(cond, w, b):
    rows, d = cond.shape
    n = w.shape[1]
    bn = 1024
    return pl.pallas_call(
        _adaln_kernel,
        grid=(n // bn,),
        in_specs=[
            pl.BlockSpec((rows, d), lambda j: (0, 0)),
            pl.BlockSpec((d, bn), lambda j: (0, j)),
            pl.BlockSpec((1, bn), lambda j: (0, j)),
        ],
        out_specs=pl.BlockSpec((rows, bn), lambda j: (0, j)),
        out_shape=jax.ShapeDtypeStruct((rows, n), F32),
        compiler_params=pltpu.CompilerParams(
            dimension_semantics=("arbitrary",), vmem_limit_bytes=VMEM_LIMIT),
        name="adaln",
    )(cond, w, b)


def _pair_block_mask():
    r = lax.broadcasted_iota(jnp.int32, (PAIR_K, PAIR_V), 0) // DK
    c = lax.broadcasted_iota(jnp.int32, (PAIR_K, PAIR_V), 1) // DV
    return r == c


def _pair_states(full, mask):
    return [jnp.where(mask, full[p * PAIR_K:(p + 1) * PAIR_K, p * PAIR_V:(p + 1) * PAIR_V], 0.0)
            for p in range(PAIRS)]


def _ctx_kernel(ctx_ref, sh_ref, sc_ref, gpre_ref, w_ref, wg_ref, bg_ref, sf_ref, sb_ref):
    n = ctx_ref.shape[1]
    hc = _rms(ctx_ref[0], gpre_ref[...]) * (1.0 + sc_ref[0]) + sh_ref[0]
    pc = _dot(hc.astype(BF16), w_ref[...])
    k = pc[:, C_K:C_K + QK]
    v = pc[:, C_V:C_V + GLA_W].astype(BF16)
    a = _dot(pc[:, C_A:C_A + LANES].astype(BF16), wg_ref[...]) + bg_ref[...]
    g = _log_sigmoid(a) * (1.0 / TAU)
    row = lax.broadcasted_iota(jnp.int32, (n, n), 0)
    col = lax.broadcasted_iota(jnp.int32, (n, n), 1)
    later = (col > row).astype(BF16)
    earlier = (col < row).astype(BF16)
    gf_hi, gf_lo = _split_bf16(g[:, :QK])
    gb_hi, gb_lo = _split_bf16(g[:, QK:])
    wf = jnp.exp(_dot(later, gf_hi) + _dot(later, gf_lo))
    wb = jnp.exp(_dot(earlier, gb_hi) + _dot(earlier, gb_lo))
    mask = _pair_block_mask()
    for p, s in enumerate(_pair_states(_dot_tn((k * wf).astype(BF16), v), mask)):
        sf_ref[0, p] = s
    for p, s in enumerate(_pair_states(_dot_tn((k * wb).astype(BF16), v), mask)):
        sb_ref[0, p] = s


def _ctx_states(ctx, mod3, g_pre, w_ctx, wg, bg):
    bsz, n, d = ctx.shape
    ctx_row = bsz * 6
    const = lambda shape: pl.BlockSpec(shape, lambda b: (0,) * len(shape))
    state = jax.ShapeDtypeStruct((bsz,) + STATE_SHAPE, F32)
    return pl.pallas_call(
        _ctx_kernel,
        grid=(bsz,),
        in_specs=[
            pl.BlockSpec((1, n, d), lambda b: (b, 0, 0)),
            pl.BlockSpec((1, 1, d), lambda b: (ctx_row + 0, 0, 0)),
            pl.BlockSpec((1, 1, d), lambda b: (ctx_row + 1, 0, 0)),
            const((1, d)),
            const((d, C_COLS)),
            const((LANES, 2 * QK)),
            const((1, 2 * QK)),
        ],
        out_specs=[pl.BlockSpec((1,) + STATE_SHAPE, lambda b: (b, 0, 0, 0))] * 2,
        out_shape=[state, state],
        compiler_params=pltpu.CompilerParams(
            dimension_semantics=("arbitrary",), vmem_limit_bytes=VMEM_LIMIT),
        name="ctx_state",
    )(ctx, mod3, mod3, g_pre, w_ctx, wg, bg)


def _gla_tile(p_ref, cols, g, state_ref, o_ref, reverse):
    col_k, col_q, col_v = cols
    n_chunks = p_ref.shape[0] // CHUNK
    ii = lax.broadcasted_iota(jnp.int32, (CHUNK, CHUNK), 0)
    jj = lax.broadcasted_iota(jnp.int32, (CHUNK, CHUNK), 1)
    i_all = lax.broadcasted_iota(jnp.int32, (CHUNK, HEADS * CHUNK), 0)
    j_all = lax.broadcasted_iota(jnp.int32, (CHUNK, HEADS * CHUNK), 1) % CHUNK
    if reverse:
        tri = (jj >= ii).astype(BF16)
        keep = j_all > i_all
        ref_row, last_row = CHUNK // 2, 0
    else:
        tri = (jj <= ii).astype(BF16)
        keep = j_all <= i_all
        ref_row, last_row = CHUNK // 2 - 1, CHUNK - 1
    tri2 = jnp.concatenate([tri, tri], axis=1)
    own_head = (lax.broadcasted_iota(jnp.int32, (HEADS * CHUNK, QK), 0) // CHUNK
                == lax.broadcasted_iota(jnp.int32, (HEADS * CHUNK, QK), 1) // DK)
    block = _pair_block_mask()

    chunks = range(n_chunks)
    order = range(n_chunks - 1, -1, -1) if reverse else range(n_chunks)
    rows = [slice(c * CHUNK, (c + 1) * CHUNK) for c in chunks]
    pair_k = [slice(p * PAIR_K, (p + 1) * PAIR_K) for p in range(PAIRS)]
    pair_v = [slice(p * PAIR_V, (p + 1) * PAIR_V) for p in range(PAIRS)]

    bs = []
    for c in chunks:
        g_hi, g_lo = _split_bf16(g[rows[c], :])
        bs.append(_dot(tri2, jnp.concatenate([g_hi, g_lo], axis=0)))

    q_ts, k_stacks, q_hs, k_hs, decays = [], [], [], [], []
    for c in chunks:
        b = bs[c]
        b_ref = b[ref_row:ref_row + 1, :]
        b_last = b[last_row:last_row + 1, :]
        e = b - b_ref
        q_t = p_ref[rows[c], col_q:col_q + QK] * (DK ** -0.5) * jnp.exp(e)
        k_t = p_ref[rows[c], col_k:col_k + QK] * jnp.exp(-e)
        q_ts.append(q_t.astype(BF16))
        k_stacks.append(
            jnp.where(own_head, jnp.concatenate([k_t] * HEADS, axis=0), 0.0).astype(BF16))
        q_hs.append((q_t * jnp.exp(b_ref)).astype(BF16))
        k_hs.append((k_t * jnp.exp(b_last - b_ref)).astype(BF16))
        decays.append(jnp.exp(b_last))

    atts = [jnp.where(keep, _dot_nt(q_ts[c], k_stacks[c]), 0.0).astype(BF16) for c in chunks]

    upds, decay_rows, v_diags = [], [], []
    for c in chunks:
        u_c, d_c, v_c = [], [], []
        for p in range(PAIRS):
            v_pair = p_ref[rows[c], col_v + p * PAIR_V:col_v + (p + 1) * PAIR_V].astype(F32)
            v_c.append(jnp.where(block, jnp.concatenate([v_pair, v_pair], axis=0), 0.0).astype(BF16))
            u_c.append(jnp.where(block, _dot_tn(k_hs[c][:, pair_k[p]], v_pair.astype(BF16)), 0.0))
            d_c.append(jnp.broadcast_to(decays[c][:, pair_k[p]], (PAIR_V, PAIR_K)).T)
        upds.append(u_c)
        decay_rows.append(d_c)
        v_diags.append(v_c)

    seen = [[None] * PAIRS for _ in chunks]
    for p in range(PAIRS):
        s = state_ref[p]
        for c in order:
            seen[c][p] = s.astype(BF16)
            s = s * decay_rows[c][p] + upds[c][p]
        state_ref[p] = s

    for c in chunks:
        for p in range(PAIRS):
            lhs = jnp.concatenate([atts[c][:, pair_k[p]], q_hs[c][:, pair_k[p]]], axis=1)
            rhs = jnp.concatenate([v_diags[c][p], seen[c][p]], axis=0)
            o_ref[rows[c], pair_v[p]] = _dot(lhs, rhs)


def _gate(a_low, wg_ref, bg_ref):
    a = _dot(a_low.astype(BF16), wg_ref[...]) + bg_ref[...]
    return _log_sigmoid(a) * (1.0 / TAU)


def _gla_bwd_kernel(x_ref, s0_ref, sh_ref, sc_ref, gpre_ref, w_ref, wg_ref, bg_ref,
                    ob_ref, kqv_ref, alow_ref, hx_ref, p_ref, state_ref):
    @pl.when(pl.program_id(1) == 0)
    def _():
        state_ref[...] = s0_ref[0]

    hx = (_rms(x_ref[0], gpre_ref[...]) * (1.0 + sc_ref[0]) + sh_ref[0]).astype(BF16)
    hx_ref[0] = hx
    half = hx.shape[0] // 2
    p_ref[:, 0:B_A] = _dot(hx, w_ref[:, 0:B_A])
    p_ref[0:half, B_A:B_COLS] = _dot(hx[0:half], w_ref[:, B_A:B_COLS])
    p_ref[half:, B_A:B_COLS] = _dot(hx[half:], w_ref[:, B_A:B_COLS])
    kqv_ref[0] = p_ref[:, 0:B_A].astype(BF16)
    alow_ref[0] = p_ref[:, B_A:B_A + LANES]
    g = _gate(p_ref[:, B_A:B_A + LANES], wg_ref, bg_ref)
    _gla_tile(p_ref, (B_K, B_Q, B_V), g, state_ref, ob_ref.at[0], reverse=True)


def _gla_bwd(x, s_b, mod3, g_pre, w_bwd, wg_b, bg_b):
    bsz, t, d = x.shape
    nt = t // TILE_SCAN
    const = lambda shape: pl.BlockSpec(shape, lambda b, i: (0,) * len(shape),
                                       pipeline_mode=pl.Buffered(1))
    return pl.pallas_call(
        _gla_bwd_kernel,
        grid=(bsz, nt),
        in_specs=[
            pl.BlockSpec((1, TILE_SCAN, d), lambda b, i: (b, nt - 1 - i, 0)),
            pl.BlockSpec((1,) + STATE_SHAPE, lambda b, i: (b, 0, 0, 0)),
            pl.BlockSpec((1, 1, d), lambda b, i: (b * 6 + 0, 0, 0)),
            pl.BlockSpec((1, 1, d), lambda b, i: (b * 6 + 1, 0, 0)),
            const((1, d)),
            const((d, B_COLS)),
            const((LANES, QK)),
            const((1, QK)),
        ],
        out_specs=[pl.BlockSpec((1, TILE_SCAN, GLA_W), lambda b, i: (b, nt - 1 - i, 0)),
                   pl.BlockSpec((1, TILE_SCAN, B_A), lambda b, i: (b, nt - 1 - i, 0)),
                   pl.BlockSpec((1, TILE_SCAN, LANES), lambda b, i: (b, nt - 1 - i, 0)),
                   pl.BlockSpec((1, TILE_SCAN, d), lambda b, i: (b, nt - 1 - i, 0))],
        out_shape=[jax.ShapeDtypeStruct((bsz, t, GLA_W), F32),
                   jax.ShapeDtypeStruct((bsz, t, B_A), BF16),
                   jax.ShapeDtypeStruct((bsz, t, LANES), F32),
                   jax.ShapeDtypeStruct((bsz, t, d), BF16)],
        scratch_shapes=[pltpu.VMEM((TILE_SCAN, B_COLS), F32), pltpu.VMEM(STATE_SHAPE, F32)],
        compiler_params=pltpu.CompilerParams(
            dimension_semantics=("arbitrary", "arbitrary"), vmem_limit_bytes=VMEM_LIMIT),
        name="gla_bwd",
    )(x, s_b, mod3, mod3, g_pre, w_bwd, wg_b, bg_b)


def _mixer_kernel(x_ref, ob_ref, kqv_ref, alow_ref, hx_ref, s0_ref, gt_ref,
                  gpost_ref, w_ref, wg_ref, bg_ref, ghead_ref, wsc_ref, bsc_ref, wout_ref,
                  x1_ref, p_ref, o_ref, state_ref):
    @pl.when(pl.program_id(1) == 0)
    def _():
        state_ref[...] = s0_ref[0]

    n = x_ref.shape[1]
    x = x_ref[0]
    p_ref[...] = _dot(hx_ref[0], w_ref[...])
    g = _gate(alow_ref[0], wg_ref, bg_ref)
    _gla_tile(kqv_ref.at[0], (B_K, B_Q, B_V), g, state_ref, o_ref, reverse=False)

    heads = []
    for h in range(HEADS):
        cols = slice(h * DV, (h + 1) * DV)
        heads.append(_rms(o_ref[:, cols] + ob_ref[0, :, cols], ghead_ref[...]))
    o_gla = jnp.concatenate(heads, axis=1) * _silu(p_ref[:, M_OG:M_OG + GLA_W])

    u = p_ref[:, M_SC:M_SC + SC_W] * p_ref[:, M_SX:M_SX + SC_W]
    pos = lax.broadcasted_iota(jnp.int32, (n, 1), 0) % GRID_W
    left = jnp.where(pos == 0, 0.0, pltpu.roll(u, 1, 0))
    right = jnp.where(pos == GRID_W - 1, 0.0, pltpu.roll(u, n - 1, 0))
    conv = wsc_ref[0:1, :] * left + wsc_ref[1:2, :] * u + wsc_ref[2:3, :] * right + bsc_ref[...]
    o_sc = p_ref[:, M_SB:M_SB + SC_W] * conv

    yx = jnp.concatenate([o_gla, o_sc], axis=1).astype(BF16)
    x1_ref[0] = x + gt_ref[0] * _rms(_dot(yx, wout_ref[...]), gpost_ref[...])


def _mixer(x, o_b, kqv, a_low, hx, s_f, mod3, g_post, w_fwd, wg_f, bg_f, g_head, w_sc, b_sc,
           w_out):
    bsz, t, d = x.shape
    nt = t // TILE_SCAN
    const = lambda shape: pl.BlockSpec(shape, lambda b, i: (0,) * len(shape),
                                       pipeline_mode=pl.Buffered(1))
    mod = lambda j: pl.BlockSpec((1, 1, d), lambda b, i: (b * 6 + j, 0, 0))
    return pl.pallas_call(
        _mixer_kernel,
        grid=(bsz, nt),
        in_specs=[
            pl.BlockSpec((1, TILE_SCAN, d), lambda b, i: (b, i, 0)),
            pl.BlockSpec((1, TILE_SCAN, GLA_W), lambda b, i: (b, i, 0)),
            pl.BlockSpec((1, TILE_SCAN, B_A), lambda b, i: (b, i, 0)),
            pl.BlockSpec((1, TILE_SCAN, LANES), lambda b, i: (b, i, 0)),
            pl.BlockSpec((1, TILE_SCAN, d), lambda b, i: (b, i, 0)),
            pl.BlockSpec((1,) + STATE_SHAPE, lambda b, i: (b, 0, 0, 0)),
            mod(2),
            const((1, d)),
            const((d, M_COLS)),
            const((LANES, QK)), const((1, QK)),
            const((1, DV)),
            const((3, SC_W)), const((1, SC_W)),
            const((d, d)),
        ],
        out_specs=pl.BlockSpec((1, TILE_SCAN, d), lambda b, i: (b, i, 0)),
        out_shape=jax.ShapeDtypeStruct((bsz, t, d), F32),
        scratch_shapes=[pltpu.VMEM((TILE_SCAN, M_COLS), F32), pltpu.VMEM((TILE_SCAN, GLA_W), F32),
                        pltpu.VMEM(STATE_SHAPE, F32)],
        compiler_params=pltpu.CompilerParams(
            dimension_semantics=("arbitrary", "arbitrary"), vmem_limit_bytes=VMEM_LIMIT),
        name="mixer",
    )(x, o_b, kqv, a_low, hx, s_f, mod3, g_post, w_fwd, wg_f, bg_f, g_head, w_sc, b_sc, w_out)


def _ffn_kernel(xm_ref, xp_ref, xn_ref, sh_ref, sc_ref, gt_ref, gpre_ref, gpost_ref,
                wup_ref, wcf_ref, bcf_ref, wdown_ref, out_ref, hx_ref, h_ref):
    i = pl.program_id(1)
    n = xm_ref.shape[1]
    ext = n + 2 * GRID_W

    def prep(v):
        return (_rms(v, gpre_ref[...]) * (1.0 + sc_ref[0]) + sh_ref[0]).astype(BF16)

    x1 = xm_ref[0]
    hx_ref[0:GRID_W, :] = prep(xp_ref[0])
    hx_ref[GRID_W:GRID_W + n, :] = prep(x1)
    hx_ref[GRID_W + n:ext, :] = prep(xn_ref[0])

    rowid = lax.broadcasted_iota(jnp.int32, (ext, 1), 0)
    has_prev = (i > 0).astype(F32)
    has_next = (i < pl.num_programs(1) - 1).astype(F32)
    pos = rowid % GRID_W

    for c in range(D_FF // MXU_N):
        cs = slice(c * MXU_N, (c + 1) * MXU_N)
        gs = slice(D_FF + c * MXU_N, D_FF + (c + 1) * MXU_N)
        u = _dot(hx_ref[...], wup_ref[:, cs])
        u = jnp.concatenate([u[0:GRID_W] * has_prev, u[GRID_W:GRID_W + n],
                             u[GRID_W + n:ext] * has_next], axis=0)
        gate = _dot(hx_ref[GRID_W:GRID_W + n, :], wup_ref[:, gs])
        u_l = jnp.where(pos == 0, 0.0, pltpu.roll(u, 1, 0)).astype(BF16)
        u_r = jnp.where(pos == GRID_W - 1, 0.0, pltpu.roll(u, ext - 1, 0)).astype(BF16)
        u_c = u.astype(BF16)
        w9 = wcf_ref[:, cs].astype(BF16)
        y = bcf_ref[:, cs].astype(BF16)
        for dr in range(3):
            rs = slice(dr * GRID_W, dr * GRID_W + n)
            y = (y + w9[3 * dr:3 * dr + 1] * u_l[rs]
                 + w9[3 * dr + 1:3 * dr + 2] * u_c[rs]
                 + w9[3 * dr + 2:3 * dr + 3] * u_r[rs])
        h_ref[:, cs] = (_silu(y.astype(F32)) * gate).astype(BF16)

    z = (_dot(h_ref[:, :FFN_DOWN_SPLIT], wdown_ref[:FFN_DOWN_SPLIT, :])
         + _dot(h_ref[:, FFN_DOWN_SPLIT:], wdown_ref[FFN_DOWN_SPLIT:, :]))
    out_ref[0] = x1 + gt_ref[0] * _rms(z, gpost_ref[...])


def _ffn(x1, mod3, g_pre, g_post, w_up, w_cf, b_cf, w_down):
    bsz, t, d = x1.shape
    nt = t // TILE_FFN
    rows_per_tile = TILE_FFN // GRID_W
    n_rows = t // GRID_W
    const = lambda shape: pl.BlockSpec(shape, lambda b, i: (0,) * len(shape),
                                       pipeline_mode=pl.Buffered(1))
    mod = lambda j: pl.BlockSpec((1, 1, d), lambda b, i: (b * 6 + j, 0, 0))
    return pl.pallas_call(
        _ffn_kernel,
        grid=(bsz, nt),
        in_specs=[
            pl.BlockSpec((1, TILE_FFN, d), lambda b, i: (b, i, 0)),
            pl.BlockSpec((1, GRID_W, d),
                         lambda b, i: (b, jnp.maximum(i * rows_per_tile - 1, 0), 0)),
            pl.BlockSpec((1, GRID_W, d),
                         lambda b, i: (b, jnp.minimum((i + 1) * rows_per_tile, n_rows - 1), 0)),
            mod(3), mod(4), mod(5),
            const((1, d)), const((1, d)),
            const((d, 2 * D_FF)),
            const((9, D_FF)), const((1, D_FF)),
            const((D_FF, d)),
        ],
        out_specs=pl.BlockSpec((1, TILE_FFN, d), lambda b, i: (b, i, 0)),
        out_shape=jax.ShapeDtypeStruct((bsz, t, d), F32),
        scratch_shapes=[pltpu.VMEM((TILE_FFN + 2 * GRID_W, d), BF16),
                        pltpu.VMEM((TILE_FFN, D_FF), BF16)],
        compiler_params=pltpu.CompilerParams(
            dimension_semantics=("arbitrary", "arbitrary"), vmem_limit_bytes=VMEM_LIMIT),
        name="ffn",
    )(x1, x1, x1, mod3, mod3, mod3, g_pre, g_post, w_up, w_cf, b_cf, w_down)


def kernel(x, c, ctx, c_ctx, w_ada, b_ada, g_pre_mix, g_post_mix, g_pre_ffn, g_post_ffn,
           w_in, w_af, b_af, w_ab, b_ab, g_head, w_sc, b_sc, w_out, w_up, w_cf, b_cf, w_down):
    bsz, t, d = x.shape
    assert d == D_MODEL and t % TILE_SCAN == 0 and t % TILE_FFN == 0 and w_in.shape[0] == 1

    cond = jnp.concatenate([c, c_ctx[None, :], jnp.zeros((16 - bsz - 1, d), F32)], axis=0)
    mod = _adaln(cond, w_ada[0], b_ada[0][None, :])
    mod3 = mod.reshape(16 * 6, 1, d)

    wi = w_in[0]
    sl = lambda a, b: wi[:, a:b]
    gate_cols = jnp.concatenate(
        [sl(COL_AF, COL_Q), jnp.zeros((d, GATE_PAD - 2 * LOWRANK), F32)], axis=1)
    w_fwd = sl(COL_OG, D_IN).astype(BF16)
    w_bwd = jnp.concatenate(
        [sl(COL_K, COL_V), sl(COL_Q, COL_OG), sl(COL_V, COL_AF), gate_cols], axis=1).astype(BF16)
    w_ctx = jnp.concatenate([sl(COL_K, COL_V), sl(COL_V, COL_AF), gate_cols], axis=1).astype(BF16)
    wg = jnp.zeros((LANES, 2 * QK), F32)
    wg = wg.at[0:LOWRANK, 0:QK].set(w_af[0]).at[LOWRANK:2 * LOWRANK, QK:].set(w_ab[0]).astype(BF16)
    bg = jnp.concatenate([b_af[0], b_ab[0]])[None, :]

    s_f, s_b = _ctx_states(ctx, mod3, g_pre_mix, w_ctx, wg, bg)
    o_b, kqv, a_low, hx = _gla_bwd(x, s_b, mod3, g_pre_mix, w_bwd, wg[:, QK:], bg[:, QK:])
    x1 = _mixer(x, o_b, kqv, a_low, hx, s_f, mod3, g_post_mix, w_fwd, wg[:, :QK],
                bg[:, :QK], g_head, w_sc[0], b_sc, w_out[0].astype(BF16))
    return _ffn(x1, mod3, g_pre_ffn, g_post_ffn, w_up[0].astype(BF16),
                w_cf[0].reshape(9, D_FF), b_cf, w_down[0].astype(BF16))
```

```python
import functools

import jax
import jax.numpy as jnp
from jax import lax
from jax.experimental import pallas as pl
from jax.experimental.pallas import tpu as pltpu

F32 = jnp.float32
BF16 = jnp.bfloat16

D_MODEL = 1024
GRID_W = 64
HEADS = 4
DK = 64
DV = 128
QK = HEADS * DK
GLA_W = HEADS * DV
LOWRANK = 16
TAU = 16.0
CHUNK = 64
SC_W = 512
D_FF = 2816
EPS = 1e-6

COL_K = 0
COL_V = COL_K + QK
COL_AF = COL_V + GLA_W
COL_AB = COL_AF + LOWRANK
COL_Q = COL_AB + LOWRANK
COL_OG = COL_Q + QK
COL_SB = COL_OG + GLA_W
COL_SC = COL_SB + SC_W
COL_SX = COL_SC + SC_W
D_IN = COL_SX + SC_W

LANES = 128
PAIRS = HEADS // 2
PAIR_K = 2 * DK
PAIR_V = 2 * DV
STATE_SHAPE = (PAIRS, PAIR_K, PAIR_V)
MXU_N = 256
FFN_DOWN_SPLIT = 8 * MXU_N
TILE_SCAN = 1024
TILE_FFN = 1024
VMEM_LIMIT = 56 * 1024 * 1024

M_OG, M_SB, M_SC, M_SX = 0, 512, 1024, 1536
M_COLS = M_SX + SC_W
GATE_PAD = LANES
B_K, B_Q, B_V, B_A = 0, 256, 512, 1024
B_COLS = B_A + GATE_PAD
C_K, C_V, C_A = 0, 256, 768
C_COLS = C_A + GATE_PAD


def _dot(a, b):
    return jnp.dot(a, b, preferred_element_type=F32)


def _dot_tn(a, b):
    return lax.dot_general(a, b, (((0,), (0,)), ((), ())), preferred_element_type=F32)


def _dot_nt(a, b):
    return lax.dot_general(a, b, (((1,), (1,)), ((), ())), preferred_element_type=F32)


def _split_bf16(x):
    hi = x.astype(BF16)
    lo = (x - hi.astype(F32)).astype(BF16)
    return hi, lo


def _rms(x, g):
    ms = jnp.mean(x * x, axis=-1, keepdims=True)
    return x * lax.rsqrt(ms + EPS) * g


def _silu(x):
    return x * jax.nn.sigmoid(x)


def _log_sigmoid(a):
    return jnp.minimum(a, 0.0) - jnp.log(1.0 + jnp.exp(-jnp.abs(a)))


def _adaln_kernel(c_ref, w_ref, b_ref, o_ref):
    s = _silu(c_ref[...])
    s_hi, s_lo = _split_bf16(s)
    w_hi, w_lo = _split_bf16(w_ref[...])
    o_ref[...] = _dot(s_hi, w_hi) + _dot(s_lo, w_hi) + _dot(s_hi, w_lo) + b_ref[...]


def _adaln(cond, w, b):
    rows, d = cond.shape
    n = w.shape[1]
    bn = 1024
    return pl.pallas_call(
        _adaln_kernel,
        grid=(n // bn,),
        in_specs=[
            pl.BlockSpec((rows, d), lambda j: (0, 0)),
            pl.BlockSpec((d, bn), lambda j: (0, j)),
            pl.BlockSpec((1, bn), lambda j: (0, j)),
        ],
        out_specs=pl.BlockSpec((rows, bn), lambda j: (0, j)),
        out_shape=jax.ShapeDtypeStruct((rows, n), F32),
        compiler_params=pltpu.CompilerParams(
            dimension_semantics=("arbitrary",), vmem_limit_bytes=VMEM_LIMIT),
        name="adaln",
    )(cond, w, b)


def _pair_block_mask():
    r = lax.broadcasted_iota(jnp.int32, (PAIR_K, PAIR_V), 0) // DK
    c = lax.broadcasted_iota(jnp.int32, (PAIR_K, PAIR_V), 1) // DV
    return r == c


def _pair_states(full, mask):
    return [jnp.where(mask, full[p * PAIR_K:(p + 1) * PAIR_K, p * PAIR_V:(p + 1) * PAIR_V], 0.0)
            for p in range(PAIRS)]


def _ctx_kernel(ctx_ref, sh_ref, sc_ref, gpre_ref, w_ref, wg_ref, bg_ref, sf_ref, sb_ref):
    n = ctx_ref.shape[1]
    hc = _rms(ctx_ref[0], gpre_ref[...] * (1.0 + sc_ref[0])) + sh_ref[0]
    pc = _dot(hc.astype(BF16), w_ref[...])
    k = pc[:, C_K:C_K + QK]
    v = pc[:, C_V:C_V + GLA_W].astype(BF16)
    a = _dot(pc[:, C_A:C_A + LANES].astype(BF16), wg_ref[...]) + bg_ref[...]
    g = _log_sigmoid(a) * (1.0 / TAU)
    row = lax.broadcasted_iota(jnp.int32, (n, n), 0)
    col = lax.broadcasted_iota(jnp.int32, (n, n), 1)
    later = (col > row).astype(BF16)
    earlier = (col < row).astype(BF16)
    gf_hi, gf_lo = _split_bf16(g[:, :QK])
    gb_hi, gb_lo = _split_bf16(g[:, QK:])
    wf = jnp.exp(_dot(later, gf_hi) + _dot(later, gf_lo))
    wb = jnp.exp(_dot(earlier, gb_hi) + _dot(earlier, gb_lo))
    mask = _pair_block_mask()
    for p, s in enumerate(_pair_states(_dot_tn((k * wf).astype(BF16), v), mask)):
        sf_ref[0, p] = s
    for p, s in enumerate(_pair_states(_dot_tn((k * wb).astype(BF16), v), mask)):
        sb_ref[0, p] = s


def _ctx_states(ctx, mod3, g_pre, w_ctx, wg, bg):
    bsz, n, d = ctx.shape
    ctx_row = bsz * 6
    const = lambda shape: pl.BlockSpec(shape, lambda b: (0,) * len(shape))
    state = jax.ShapeDtypeStruct((bsz,) + STATE_SHAPE, F32)
    return pl.pallas_call(
        _ctx_kernel,
        grid=(bsz,),
        in_specs=[
            pl.BlockSpec((1, n, d), lambda b: (b, 0, 0)),
            pl.BlockSpec((1, 1, d), lambda b: (ctx_row + 0, 0, 0)),
            pl.BlockSpec((1, 1, d), lambda b: (ctx_row + 1, 0, 0)),
            const((1, d)),
            const((d, C_COLS)),
            const((LANES, 2 * QK)),
            const((1, 2 * QK)),
        ],
        out_specs=[pl.BlockSpec((1,) + STATE_SHAPE, lambda b: (b, 0, 0, 0))] * 2,
        out_shape=[state, state],
        compiler_params=pltpu.CompilerParams(
            dimension_semantics=("arbitrary",), vmem_limit_bytes=VMEM_LIMIT),
        name="ctx_state",
    )(ctx, mod3, mod3, g_pre, w_ctx, wg, bg)


def _gla_tile(p_ref, cols, g, state_ref, o_ref, reverse):
    col_k, col_q, col_v = cols
    n_chunks = p_ref.shape[0] // CHUNK
    ii = lax.broadcasted_iota(jnp.int32, (CHUNK, CHUNK), 0)
    jj = lax.broadcasted_iota(jnp.int32, (CHUNK, CHUNK), 1)
    i_all = lax.broadcasted_iota(jnp.int32, (CHUNK, HEADS * CHUNK), 0)
    j_all = lax.broadcasted_iota(jnp.int32, (CHUNK, HEADS * CHUNK), 1) % CHUNK
    if reverse:
        tri = (jj >= ii).astype(BF16)
        keep = j_all > i_all
        ref_row, last_row = CHUNK // 2, 0
    else:
        tri = (jj <= ii).astype(BF16)
        keep = j_all <= i_all
        ref_row, last_row = CHUNK // 2 - 1, CHUNK - 1
    tri2 = jnp.concatenate([tri, tri], axis=1)
    own_head = (lax.broadcasted_iota(jnp.int32, (HEADS * CHUNK, QK), 0) // CHUNK
                == lax.broadcasted_iota(jnp.int32, (HEADS * CHUNK, QK), 1) // DK)
    block = _pair_block_mask()

    chunks = range(n_chunks)
    order = range(n_chunks - 1, -1, -1) if reverse else range(n_chunks)
    rows = [slice(c * CHUNK, (c + 1) * CHUNK) for c in chunks]
    pair_k = [slice(p * PAIR_K, (p + 1) * PAIR_K) for p in range(PAIRS)]
    pair_v = [slice(p * PAIR_V, (p + 1) * PAIR_V) for p in range(PAIRS)]

    bs = []
    for c in chunks:
        g_hi, g_lo = _split_bf16(g[rows[c], :])
        bs.append(_dot(tri2, jnp.concatenate([g_hi, g_lo], axis=0)))

    q_ts, k_stacks, q_hs, k_hs, decays = [], [], [], [], []
    for c in chunks:
        b = bs[c]
        b_ref = b[ref_row:ref_row + 1, :]
        b_last = b[last_row:last_row + 1, :]
        e = b - b_ref
        q_t = p_ref[rows[c], col_q:col_q + QK] * (DK ** -0.5) * jnp.exp(e)
        k_t = p_ref[rows[c], col_k:col_k + QK] * jnp.exp(-e)
        q_ts.append(q_t.astype(BF16))
        k_stacks.append(
            jnp.where(own_head, jnp.concatenate([k_t] * HEADS, axis=0), 0.0).astype(BF16))
        q_hs.append((q_t * jnp.exp(b_ref)).astype(BF16))
        k_hs.append((k_t * jnp.exp(b_last - b_ref)).astype(BF16))
        decays.append(jnp.exp(b_last))

    atts = [jnp.where(keep, _dot_nt(q_ts[c], k_stacks[c]), 0.0).astype(BF16) for c in chunks]

    upds, decay_rows, v_diags = [], [], []
    for c in chunks:
        u_c, d_c, v_c = [], [], []
        for p in range(PAIRS):
            v_pair = p_ref[rows[c], col_v + p * PAIR_V:col_v + (p + 1) * PAIR_V].astype(F32)
            v_c.append(jnp.where(block, jnp.concatenate([v_pair, v_pair], axis=0), 0.0).astype(BF16))
            u_c.append(jnp.where(block, _dot_tn(k_hs[c][:, pair_k[p]], v_pair.astype(BF16)), 0.0))
            d_c.append(jnp.broadcast_to(decays[c][:, pair_k[p]], (PAIR_V, PAIR_K)).T)
        upds.append(u_c)
        decay_rows.append(d_c)
        v_diags.append(v_c)

    seen = [[None] * PAIRS for _ in chunks]
    for p in range(PAIRS):
        s = state_ref[p]
        for c in order:
            seen[c][p] = s.astype(BF16)
            s = s * decay_rows[c][p] + upds[c][p]
        state_ref[p] = s

    for c in chunks:
        for p in range(PAIRS):
            lhs = jnp.concatenate([atts[c][:, pair_k[p]], q_hs[c][:, pair_k[p]]], axis=1)
            rhs = jnp.concatenate([v_diags[c][p], seen[c][p]], axis=0)
            o_ref[rows[c], pair_v[p]] = _dot(lhs, rhs)


def _gate(a_low, wg_ref, bg_ref):
    a = _dot(a_low.astype(BF16), wg_ref[...]) + bg_ref[...]
    return _log_sigmoid(a) * (1.0 / TAU)


def _gla_bwd_kernel(x_ref, s0_ref, sh_ref, sc_ref, gpre_ref, w_ref, wg_ref, bg_ref,
                    ob_ref, kqv_ref, alow_ref, hx_ref, p_ref, state_ref):
    @pl.when(pl.program_id(1) == 0)
    def _():
        state_ref[...] = s0_ref[0]

    hx = (_rms(x_ref[0], gpre_ref[...] * (1.0 + sc_ref[0])) + sh_ref[0]).astype(BF16)
    hx_ref[0] = hx
    half = hx.shape[0] // 2
    p_ref[:, 0:B_A] = _dot(hx, w_ref[:, 0:B_A])
    p_ref[0:half, B_A:B_COLS] = _dot(hx[0:half], w_ref[:, B_A:B_COLS])
    p_ref[half:, B_A:B_COLS] = _dot(hx[half:], w_ref[:, B_A:B_COLS])
    kqv_ref[0] = p_ref[:, 0:B_A].astype(BF16)
    alow_ref[0] = p_ref[:, B_A:B_A + LANES]
    g = _gate(p_ref[:, B_A:B_A + LANES], wg_ref, bg_ref)
    _gla_tile(p_ref, (B_K, B_Q, B_V), g, state_ref, ob_ref.at[0], reverse=True)


def _gla_bwd(x, s_b, mod3, g_pre, w_bwd, wg_b, bg_b):
    bsz, t, d = x.shape
    nt = t // TILE_SCAN
    const = lambda shape: pl.BlockSpec(shape, lambda b, i: (0,) * len(shape),
                                       pipeline_mode=pl.Buffered(1))
    return pl.pallas_call(
        _gla_bwd_kernel,
        grid=(bsz, nt),
        in_specs=[
            pl.BlockSpec((1, TILE_SCAN, d), lambda b, i: (b, nt - 1 - i, 0)),
            pl.BlockSpec((1,) + STATE_SHAPE, lambda b, i: (b, 0, 0, 0)),
            pl.BlockSpec((1, 1, d), lambda b, i: (b * 6 + 0, 0, 0)),
            pl.BlockSpec((1, 1, d), lambda b, i: (b * 6 + 1, 0, 0)),
            const((1, d)),
            const((d, B_COLS)),
            const((LANES, QK)),
            const((1, QK)),
        ],
        out_specs=[pl.BlockSpec((1, TILE_SCAN, GLA_W), lambda b, i: (b, nt - 1 - i, 0)),
                   pl.BlockSpec((1, TILE_SCAN, B_A), lambda b, i: (b, nt - 1 - i, 0)),
                   pl.BlockSpec((1, TILE_SCAN, LANES), lambda b, i: (b, nt - 1 - i, 0)),
                   pl.BlockSpec((1, TILE_SCAN, d), lambda b, i: (b, nt - 1 - i, 0))],
        out_shape=[jax.ShapeDtypeStruct((bsz, t, GLA_W), F32),
                   jax.ShapeDtypeStruct((bsz, t, B_A), BF16),
                   jax.ShapeDtypeStruct((bsz, t, LANES), F32),
                   jax.ShapeDtypeStruct((bsz, t, d), BF16)],
        scratch_shapes=[pltpu.VMEM((TILE_SCAN, B_COLS), F32), pltpu.VMEM(STATE_SHAPE, F32)],
        compiler_params=pltpu.CompilerParams(
            dimension_semantics=("arbitrary", "arbitrary"), vmem_limit_bytes=VMEM_LIMIT),
        name="gla_bwd",
    )(x, s_b, mod3, mod3, g_pre, w_bwd, wg_b, bg_b)


def _mixer_kernel(x_ref, ob_ref, kqv_ref, alow_ref, hx_ref, s0_ref, gt_ref,
                  gpost_ref, w_ref, wg_ref, bg_ref, ghead_ref, wsc_ref, bsc_ref, wout_ref,
                  x1_ref, p_ref, o_ref, state_ref):
    @pl.when(pl.program_id(1) == 0)
    def _():
        state_ref[...] = s0_ref[0]

    n = x_ref.shape[1]
    x = x_ref[0]
    p_ref[...] = _dot(hx_ref[0], w_ref[...])
    g = _gate(alow_ref[0], wg_ref, bg_ref)
    _gla_tile(kqv_ref.at[0], (B_K, B_Q, B_V), g, state_ref, o_ref, reverse=False)

    heads = []
    for h in range(HEADS):
        cols = slice(h * DV, (h + 1) * DV)
        heads.append(_rms(o_ref[:, cols] + ob_ref[0, :, cols], ghead_ref[...]))
    o_gla = jnp.concatenate(heads, axis=1) * _silu(p_ref[:, M_OG:M_OG + GLA_W])

    u = p_ref[:, M_SC:M_SC + SC_W] * p_ref[:, M_SX:M_SX + SC_W]
    pos = lax.broadcasted_iota(jnp.int32, (n, 1), 0) % GRID_W
    left = jnp.where(pos == 0, 0.0, pltpu.roll(u, 1, 0))
    right = jnp.where(pos == GRID_W - 1, 0.0, pltpu.roll(u, n - 1, 0))
    conv = wsc_ref[0:1, :] * left + wsc_ref[1:2, :] * u + wsc_ref[2:3, :] * right + bsc_ref[...]
    o_sc = p_ref[:, M_SB:M_SB + SC_W] * conv

    yx = jnp.concatenate([o_gla, o_sc], axis=1).astype(BF16)
    x1_ref[0] = x + _rms(_dot(yx, wout_ref[...]), gpost_ref[...] * gt_ref[0])


def _mixer(x, o_b, kqv, a_low, hx, s_f, mod3, g_post, w_fwd, wg_f, bg_f, g_head, w_sc, b_sc,
           w_out):
    bsz, t, d = x.shape
    nt = t // TILE_SCAN
    const = lambda shape: pl.BlockSpec(shape, lambda b, i: (0,) * len(shape),
                                       pipeline_mode=pl.Buffered(1))
    mod = lambda j: pl.BlockSpec((1, 1, d), lambda b, i: (b * 6 + j, 0, 0))
    return pl.pallas_call(
        _mixer_kernel,
        grid=(bsz, nt),
        in_specs=[
            pl.BlockSpec((1, TILE_SCAN, d), lambda b, i: (b, i, 0)),
            pl.BlockSpec((1, TILE_SCAN, GLA_W), lambda b, i: (b, i, 0)),
            pl.BlockSpec((1, TILE_SCAN, B_A), lambda b, i: (b, i, 0)),
            pl.BlockSpec((1, TILE_SCAN, LANES), lambda b, i: (b, i, 0)),
            pl.BlockSpec((1, TILE_SCAN, d), lambda b, i: (b, i, 0)),
            pl.BlockSpec((1,) + STATE_SHAPE, lambda b, i: (b, 0, 0, 0)),
            mod(2),
            const((1, d)),
            const((d, M_COLS)),
            const((LANES, QK)), const((1, QK)),
            const((1, DV)),
            const((3, SC_W)), const((1, SC_W)),
            const((d, d)),
        ],
        out_specs=pl.BlockSpec((1, TILE_SCAN, d), lambda b, i: (b, i, 0)),
        out_shape=jax.ShapeDtypeStruct((bsz, t, d), F32),
        scratch_shapes=[pltpu.VMEM((TILE_SCAN, M_COLS), F32), pltpu.VMEM((TILE_SCAN, GLA_W), F32),
                        pltpu.VMEM(STATE_SHAPE, F32)],
        compiler_params=pltpu.CompilerParams(
            dimension_semantics=("arbitrary", "arbitrary"), vmem_limit_bytes=VMEM_LIMIT),
        name="mixer",
    )(x, o_b, kqv, a_low, hx, s_f, mod3, g_post, w_fwd, wg_f, bg_f, g_head, w_sc, b_sc, w_out)


def _ffn_kernel(xm_ref, xp_ref, xn_ref, sh_ref, sc_ref, gt_ref, gpre_ref, gpost_ref,
                wup_ref, wcf_ref, bcf_ref, wdown_ref, out_ref, hx_ref, h_ref):
    i = pl.program_id(1)
    n = xm_ref.shape[1]
    ext = n + 2 * GRID_W

    def prep(v):
        return (_rms(v, gpre_ref[...] * (1.0 + sc_ref[0])) + sh_ref[0]).astype(BF16)

    x1 = xm_ref[0]
    hx_ref[0:GRID_W, :] = prep(xp_ref[0])
    hx_ref[GRID_W:GRID_W + n, :] = prep(x1)
    hx_ref[GRID_W + n:ext, :] = prep(xn_ref[0])

    rowid = lax.broadcasted_iota(jnp.int32, (ext, 1), 0)
    has_prev = (i > 0).astype(F32)
    has_next = (i < pl.num_programs(1) - 1).astype(F32)
    pos = rowid % GRID_W

    for c in range(D_FF // MXU_N):
        cs = slice(c * MXU_N, (c + 1) * MXU_N)
        gs = slice(D_FF + c * MXU_N, D_FF + (c + 1) * MXU_N)
        u = _dot(hx_ref[...], wup_ref[:, cs])
        u = jnp.concatenate([u[0:GRID_W] * has_prev, u[GRID_W:GRID_W + n],
                             u[GRID_W + n:ext] * has_next], axis=0)
        gate = _dot(hx_ref[GRID_W:GRID_W + n, :], wup_ref[:, gs])
        u_l = jnp.where(pos == 0, 0.0, pltpu.roll(u, 1, 0)).astype(BF16)
        u_r = jnp.where(pos == GRID_W - 1, 0.0, pltpu.roll(u, ext - 1, 0)).astype(BF16)
        u_c = u.astype(BF16)
        w9 = wcf_ref[:, cs].astype(BF16)
        y = bcf_ref[:, cs].astype(BF16)
        for dr in range(3):
            rs = slice(dr * GRID_W, dr * GRID_W + n)
            y = (y + w9[3 * dr:3 * dr + 1] * u_l[rs]
                 + w9[3 * dr + 1:3 * dr + 2] * u_c[rs]
                 + w9[3 * dr + 2:3 * dr + 3] * u_r[rs])
        h_ref[:, cs] = (_silu(y.astype(F32)) * gate).astype(BF16)

    z = (_dot(h_ref[:, :FFN_DOWN_SPLIT], wdown_ref[:FFN_DOWN_SPLIT, :])
         + _dot(h_ref[:, FFN_DOWN_SPLIT:], wdown_ref[FFN_DOWN_SPLIT:, :]))
    out_ref[0] = x1 + _rms(z, gpost_ref[...] * gt_ref[0])


def _ffn(x1, mod3, g_pre, g_post, w_up, w_cf, b_cf, w_down):
    bsz, t, d = x1.shape
    nt = t // TILE_FFN
    rows_per_tile = TILE_FFN // GRID_W
    n_rows = t // GRID_W
    const = lambda shape: pl.BlockSpec(shape, lambda b, i: (0,) * len(shape),
                                       pipeline_mode=pl.Buffered(1))
    mod = lambda j: pl.BlockSpec((1, 1, d), lambda b, i: (b * 6 + j, 0, 0))
    return pl.pallas_call(
        _ffn_kernel,
        grid=(bsz, nt),
        in_specs=[
            pl.BlockSpec((1, TILE_FFN, d), lambda b, i: (b, i, 0)),
            pl.BlockSpec((1, GRID_W, d),
                         lambda b, i: (b, jnp.maximum(i * rows_per_tile - 1, 0), 0)),
            pl.BlockSpec((1, GRID_W, d),
                         lambda b, i: (b, jnp.minimum((i + 1) * rows_per_tile, n_rows - 1), 0)),
            mod(3), mod(4), mod(5),
            const((1, d)), const((1, d)),
            const((d, 2 * D_FF)),
            const((9, D_FF)), const((1, D_FF)),
            const((D_FF, d)),
        ],
        out_specs=pl.BlockSpec((1, TILE_FFN, d), lambda b, i: (b, i, 0)),
        out_shape=jax.ShapeDtypeStruct((bsz, t, d), F32),
        scratch_shapes=[pltpu.VMEM((TILE_FFN + 2 * GRID_W, d), BF16),
                        pltpu.VMEM((TILE_FFN, D_FF), BF16)],
        compiler_params=pltpu.CompilerParams(
            dimension_semantics=("arbitrary", "arbitrary"), vmem_limit_bytes=VMEM_LIMIT),
        name="ffn",
    )(x1, x1, x1, mod3, mod3, mod3, g_pre, g_post, w_up, w_cf, b_cf, w_down)


def kernel(x, c, ctx, c_ctx, w_ada, b_ada, g_pre_mix, g_post_mix, g_pre_ffn, g_post_ffn,
           w_in, w_af, b_af, w_ab, b_ab, g_head, w_sc, b_sc, w_out, w_up, w_cf, b_cf, w_down):
    bsz, t, d = x.shape
    assert d == D_MODEL and t % TILE_SCAN == 0 and t % TILE_FFN == 0 and w_in.shape[0] == 1

    cond = jnp.concatenate([c, c_ctx[None, :], jnp.zeros((16 - bsz - 1, d), F32)], axis=0)
    mod = _adaln(cond, w_ada[0], b_ada[0][None, :])
    mod3 = mod.reshape(16 * 6, 1, d)

    wi = w_in[0]
    sl = lambda a, b: wi[:, a:b]
    gate_cols = jnp.concatenate(
        [sl(COL_AF, COL_Q), jnp.zeros((d, GATE_PAD - 2 * LOWRANK), F32)], axis=1)
    w_fwd = sl(COL_OG, D_IN).astype(BF16)
    w_bwd = jnp.concatenate(
        [sl(COL_K, COL_V), sl(COL_Q, COL_OG), sl(COL_V, COL_AF), gate_cols], axis=1).astype(BF16)
    w_ctx = jnp.concatenate([sl(COL_K, COL_V), sl(COL_V, COL_AF), gate_cols], axis=1).astype(BF16)
    wg = jnp.zeros((LANES, 2 * QK), F32)
    wg = wg.at[0:LOWRANK, 0:QK].set(w_af[0]).at[LOWRANK:2 * LOWRANK, QK:].set(w_ab[0]).astype(BF16)
    bg = jnp.concatenate([b_af[0], b_ab[0]])[None, :]

    s_f, s_b = _ctx_states(ctx, mod3, g_pre_mix, w_ctx, wg, bg)
    o_b, kqv, a_low, hx = _gla_bwd(x, s_b, mod3, g_pre_mix, w_bwd, wg[:, QK:], bg[:, QK:])
    x1 = _mixer(x, o_b, kqv, a_low, hx, s_f, mod3, g_post_mix, w_fwd, wg[:, :QK],
                bg[:, :QK], g_head, w_sc[0], b_sc, w_out[0].astype(BF16))
    return _ffn(x1, mod3, g_pre_ffn, g_post_ffn, w_up[0].astype(BF16),
                w_cf[0].reshape(9, D_FF), b_cf, w_down[0].astype(BF16))
```

```python
import functools

import jax
import jax.numpy as jnp
from jax import lax
from jax.experimental import pallas as pl
from jax.experimental.pallas import tpu as pltpu

F32 = jnp.float32
BF16 = jnp.bfloat16

D_MODEL = 1024
GRID_W = 64
HEADS = 4
DK = 64
DV = 128
QK = HEADS * DK
GLA_W = HEADS * DV
LOWRANK = 16
TAU = 16.0
CHUNK = 64
SC_W = 512
D_FF = 2816
EPS = 1e-6

COL_K = 0
COL_V = COL_K + QK
COL_AF = COL_V + GLA_W
COL_AB = COL_AF + LOWRANK
COL_Q = COL_AB + LOWRANK
COL_OG = COL_Q + QK
COL_SB = COL_OG + GLA_W
COL_SC = COL_SB + SC_W
COL_SX = COL_SC + SC_W
D_IN = COL_SX + SC_W

LANES = 128
PAIRS = HEADS // 2
PAIR_K = 2 * DK
PAIR_V = 2 * DV
STATE_SHAPE = (PAIRS, PAIR_K, PAIR_V)
MXU_N = 256
FFN_DOWN_SPLIT = 8 * MXU_N
TILE_SCAN = 1024
TILE_FFN = 1024
VMEM_LIMIT = 56 * 1024 * 1024

M_OG, M_SB, M_SC, M_SX = 0, 512, 1024, 1536
M_COLS = M_SX + SC_W
GATE_PAD = LANES
B_K, B_Q, B_V, B_A = 0, 256, 512, 1024
B_COLS = B_A + GATE_PAD
C_K, C_V, C_A = 0, 256, 768
C_COLS = C_A + GATE_PAD


def _dot(a, b):
    return jnp.dot(a, b, preferred_element_type=F32)


def _dot_tn(a, b):
    return lax.dot_general(a, b, (((0,), (0,)), ((), ())), preferred_element_type=F32)


def _dot_nt(a, b):
    return lax.dot_general(a, b, (((1,), (1,)), ((), ())), preferred_element_type=F32)


def _split_bf16(x):
    hi = x.astype(BF16)
    lo = (x - hi.astype(F32)).astype(BF16)
    return hi, lo


def _rms(x, g):
    ms = jnp.mean(x * x, axis=-1, keepdims=True)
    return x * lax.rsqrt(ms + EPS) * g


def _silu(x):
    return x * jax.nn.sigmoid(x)


def _log_sigmoid(a):
    return jnp.minimum(a, 0.0) - jnp.log(1.0 + jnp.exp(-jnp.abs(a)))


def _adaln_kernel(c_ref, w_ref, b_ref, o_ref):
    s = _silu(c_ref[...])
    s_hi, s_lo = _split_bf16(s)
    w_hi, w_lo = _split_bf16(w_ref[...])
    o_ref[...] = _dot(s_hi, w_hi) + _dot(s_lo, w_hi) + _dot(s_hi, w_lo) + b_ref[...]


def _adaln(cond, w, b):
    rows, d = cond.shape
    n = w.shape[1]
    bn = 1024
    return pl.pallas_call(
        _adaln_kernel,
        grid=(n // bn,),
        in_specs=[
            pl.BlockSpec((rows, d), lambda j: (0, 0)),
            pl.BlockSpec((d, bn), lambda j: (0, j)),
            pl.BlockSpec((1, bn), lambda j: (0, j)),
        ],
        out_specs=pl.BlockSpec((rows, bn), lambda j: (0, j)),
        out_shape=jax.ShapeDtypeStruct((rows, n), F32),
        compiler_params=pltpu.CompilerParams(
            dimension_semantics=("arbitrary",), vmem_limit_bytes=VMEM_LIMIT),
        name="adaln",
    )(cond, w, b)


def _pair_block_mask():
    r = lax.broadcasted_iota(jnp.int32, (PAIR_K, PAIR_V), 0) // DK
    c = lax.broadcasted_iota(jnp.int32, (PAIR_K, PAIR_V), 1) // DV
    return r == c


def _pair_states(full, mask):
    return [jnp.where(mask, full[p * PAIR_K:(p + 1) * PAIR_K, p * PAIR_V:(p + 1) * PAIR_V], 0.0)
            for p in range(PAIRS)]


def _ctx_kernel(ctx_ref, sh_ref, sc_ref, gpre_ref, w_ref, wg_ref, bg_ref, sf_ref, sb_ref):
    n = ctx_ref.shape[1]
    hc = _rms(ctx_ref[0], gpre_ref[...] * (1.0 + sc_ref[0])) + sh_ref[0]
    pc = _dot(hc.astype(BF16), w_ref[...])
    k = pc[:, C_K:C_K + QK]
    v = pc[:, C_V:C_V + GLA_W].astype(BF16)
    a = _dot(pc[:, C_A:C_A + LANES].astype(BF16), wg_ref[...]) + bg_ref[...]
    g = _log_sigmoid(a) * (1.0 / TAU)
    row = lax.broadcasted_iota(jnp.int32, (n, n), 0)
    col = lax.broadcasted_iota(jnp.int32, (n, n), 1)
    later = (col > row).astype(BF16)
    earlier = (col < row).astype(BF16)
    gf_hi, gf_lo = _split_bf16(g[:, :QK])
    gb_hi, gb_lo = _split_bf16(g[:, QK:])
    wf = jnp.exp(_dot(later, gf_hi) + _dot(later, gf_lo))
    wb = jnp.exp(_dot(earlier, gb_hi) + _dot(earlier, gb_lo))
    mask = _pair_block_mask()
    for p, s in enumerate(_pair_states(_dot_tn((k * wf).astype(BF16), v), mask)):
        sf_ref[0, p] = s
    for p, s in enumerate(_pair_states(_dot_tn((k * wb).astype(BF16), v), mask)):
        sb_ref[0, p] = s


def _ctx_states(ctx, mod3, g_pre, w_ctx, wg, bg):
    bsz, n, d = ctx.shape
    ctx_row = bsz * 6
    const = lambda shape: pl.BlockSpec(shape, lambda b: (0,) * len(shape))
    state = jax.ShapeDtypeStruct((bsz,) + STATE_SHAPE, F32)
    return pl.pallas_call(
        _ctx_kernel,
        grid=(bsz,),
        in_specs=[
            pl.BlockSpec((1, n, d), lambda b: (b, 0, 0)),
            pl.BlockSpec((1, 1, d), lambda b: (ctx_row + 0, 0, 0)),
            pl.BlockSpec((1, 1, d), lambda b: (ctx_row + 1, 0, 0)),
            const((1, d)),
            const((d, C_COLS)),
            const((LANES, 2 * QK)),
            const((1, 2 * QK)),
        ],
        out_specs=[pl.BlockSpec((1,) + STATE_SHAPE, lambda b: (b, 0, 0, 0))] * 2,
        out_shape=[state, state],
        compiler_params=pltpu.CompilerParams(
            dimension_semantics=("arbitrary",), vmem_limit_bytes=VMEM_LIMIT),
        name="ctx_state",
    )(ctx, mod3, mod3, g_pre, w_ctx, wg, bg)


def _gla_tile(p_ref, cols, g, state_ref, o_ref, reverse):
    col_k, col_q, col_v = cols
    n_chunks = p_ref.shape[0] // CHUNK
    ii = lax.broadcasted_iota(jnp.int32, (CHUNK, CHUNK), 0)
    jj = lax.broadcasted_iota(jnp.int32, (CHUNK, CHUNK), 1)
    i_all = lax.broadcasted_iota(jnp.int32, (CHUNK, HEADS * CHUNK), 0)
    j_all = lax.broadcasted_iota(jnp.int32, (CHUNK, HEADS * CHUNK), 1) % CHUNK
    if reverse:
        tri = (jj >= ii).astype(BF16)
        keep = j_all > i_all
        ref_row, last_row = CHUNK // 2, 0
    else:
        tri = (jj <= ii).astype(BF16)
        keep = j_all <= i_all
        ref_row, last_row = CHUNK // 2 - 1, CHUNK - 1
    tri2 = jnp.concatenate([tri, tri], axis=1)
    own_head = (lax.broadcasted_iota(jnp.int32, (HEADS * CHUNK, QK), 0) // CHUNK
                == lax.broadcasted_iota(jnp.int32, (HEADS * CHUNK, QK), 1) // DK)
    block = _pair_block_mask()
    own_head_bf = own_head.astype(BF16)
    block_bf = block.astype(BF16)

    chunks = range(n_chunks)
    order = range(n_chunks - 1, -1, -1) if reverse else range(n_chunks)
    rows = [slice(c * CHUNK, (c + 1) * CHUNK) for c in chunks]
    pair_k = [slice(p * PAIR_K, (p + 1) * PAIR_K) for p in range(PAIRS)]
    pair_v = [slice(p * PAIR_V, (p + 1) * PAIR_V) for p in range(PAIRS)]

    bs = []
    for c in chunks:
        g_hi, g_lo = _split_bf16(g[rows[c], :])
        bs.append(_dot(tri2, jnp.concatenate([g_hi, g_lo], axis=0)))

    q_ts, k_stacks, q_hs, k_hs, decays = [], [], [], [], []
    for c in chunks:
        b = bs[c]
        b_ref = b[ref_row:ref_row + 1, :]
        b_last = b[last_row:last_row + 1, :]
        e = b - b_ref
        q_t = p_ref[rows[c], col_q:col_q + QK] * (DK ** -0.5) * jnp.exp(e)
        k_t = p_ref[rows[c], col_k:col_k + QK] * jnp.exp(-e)
        q_ts.append(q_t.astype(BF16))
        k_stacks.append(jnp.concatenate([k_t.astype(BF16)] * HEADS, axis=0) * own_head_bf)
        q_hs.append((q_t * jnp.exp(b_ref)).astype(BF16))
        k_hs.append((k_t * jnp.exp(b_last - b_ref)).astype(BF16))
        decays.append(jnp.exp(b_last))

    atts = [jnp.where(keep, _dot_nt(q_ts[c], k_stacks[c]), 0.0).astype(BF16) for c in chunks]

    upds, decay_rows, v_diags = [], [], []
    for c in chunks:
        u_c, d_c, v_c = [], [], []
        for p in range(PAIRS):
            v_pair = p_ref[rows[c], col_v + p * PAIR_V:col_v + (p + 1) * PAIR_V].astype(BF16)
            v_c.append(jnp.concatenate([v_pair, v_pair], axis=0) * block_bf)
            u_c.append(jnp.where(block, _dot_tn(k_hs[c][:, pair_k[p]], v_pair), 0.0))
            d_c.append(jnp.broadcast_to(decays[c][:, pair_k[p]], (PAIR_V, PAIR_K)).T)
        upds.append(u_c)
        decay_rows.append(d_c)
        v_diags.append(v_c)

    seen = [[None] * PAIRS for _ in chunks]
    for p in range(PAIRS):
        s = state_ref[p]
        for c in order:
            seen[c][p] = s.astype(BF16)
            s = s * decay_rows[c][p] + upds[c][p]
        state_ref[p] = s

    for c in chunks:
        for p in range(PAIRS):
            lhs = jnp.concatenate([atts[c][:, pair_k[p]], q_hs[c][:, pair_k[p]]], axis=1)
            rhs = jnp.concatenate([v_diags[c][p], seen[c][p]], axis=0)
            o_ref[rows[c], pair_v[p]] = _dot(lhs, rhs)


def _gate(a_low, wg_ref, bg_ref):
    a = _dot(a_low.astype(BF16), wg_ref[...]) + bg_ref[...]
    return _log_sigmoid(a) * (1.0 / TAU)


def _gla_bwd_kernel(x_ref, s0_ref, sh_ref, sc_ref, gpre_ref, w_ref, wg_ref, bg_ref,
                    ob_ref, kqv_ref, alow_ref, hx_ref, p_ref, state_ref):
    @pl.when(pl.program_id(1) == 0)
    def _():
        state_ref[...] = s0_ref[0]

    hx = (_rms(x_ref[0], gpre_ref[...] * (1.0 + sc_ref[0])) + sh_ref[0]).astype(BF16)
    hx_ref[0] = hx
    half = hx.shape[0] // 2
    p_ref[:, 0:B_A] = _dot(hx, w_ref[:, 0:B_A])
    p_ref[0:half, B_A:B_COLS] = _dot(hx[0:half], w_ref[:, B_A:B_COLS])
    p_ref[half:, B_A:B_COLS] = _dot(hx[half:], w_ref[:, B_A:B_COLS])
    kqv_ref[0] = p_ref[:, 0:B_A].astype(BF16)
    alow_ref[0] = p_ref[:, B_A:B_A + LANES]
    g = _gate(p_ref[:, B_A:B_A + LANES], wg_ref, bg_ref)
    _gla_tile(p_ref, (B_K, B_Q, B_V), g, state_ref, ob_ref.at[0], reverse=True)


def _gla_bwd(x, s_b, mod3, g_pre, w_bwd, wg_b, bg_b):
    bsz, t, d = x.shape
    nt = t // TILE_SCAN
    const = lambda shape: pl.BlockSpec(shape, lambda b, i: (0,) * len(shape),
                                       pipeline_mode=pl.Buffered(1))
    return pl.pallas_call(
        _gla_bwd_kernel,
        grid=(bsz, nt),
        in_specs=[
            pl.BlockSpec((1, TILE_SCAN, d), lambda b, i: (b, nt - 1 - i, 0)),
            pl.BlockSpec((1,) + STATE_SHAPE, lambda b, i: (b, 0, 0, 0)),
            pl.BlockSpec((1, 1, d), lambda b, i: (b * 6 + 0, 0, 0)),
            pl.BlockSpec((1, 1, d), lambda b, i: (b * 6 + 1, 0, 0)),
            const((1, d)),
            const((d, B_COLS)),
            const((LANES, QK)),
            const((1, QK)),
        ],
        out_specs=[pl.BlockSpec((1, TILE_SCAN, GLA_W), lambda b, i: (b, nt - 1 - i, 0)),
                   pl.BlockSpec((1, TILE_SCAN, B_A), lambda b, i: (b, nt - 1 - i, 0)),
                   pl.BlockSpec((1, TILE_SCAN, LANES), lambda b, i: (b, nt - 1 - i, 0)),
                   pl.BlockSpec((1, TILE_SCAN, d), lambda b, i: (b, nt - 1 - i, 0))],
        out_shape=[jax.ShapeDtypeStruct((bsz, t, GLA_W), F32),
                   jax.ShapeDtypeStruct((bsz, t, B_A), BF16),
                   jax.ShapeDtypeStruct((bsz, t, LANES), F32),
                   jax.ShapeDtypeStruct((bsz, t, d), BF16)],
        scratch_shapes=[pltpu.VMEM((TILE_SCAN, B_COLS), F32), pltpu.VMEM(STATE_SHAPE, F32)],
        compiler_params=pltpu.CompilerParams(
            dimension_semantics=("arbitrary", "arbitrary"), vmem_limit_bytes=VMEM_LIMIT),
        name="gla_bwd",
    )(x, s_b, mod3, mod3, g_pre, w_bwd, wg_b, bg_b)


def _mixer_kernel(x_ref, ob_ref, kqv_ref, alow_ref, hx_ref, s0_ref, gt_ref,
                  gpost_ref, w_ref, wg_ref, bg_ref, ghead_ref, wsc_ref, bsc_ref, wout_ref,
                  x1_ref, p_ref, o_ref, state_ref):
    @pl.when(pl.program_id(1) == 0)
    def _():
        state_ref[...] = s0_ref[0]

    n = x_ref.shape[1]
    x = x_ref[0]
    p_ref[...] = _dot(hx_ref[0], w_ref[...])
    g = _gate(alow_ref[0], wg_ref, bg_ref)
    _gla_tile(kqv_ref.at[0], (B_K, B_Q, B_V), g, state_ref, o_ref, reverse=False)

    heads = []
    for h in range(HEADS):
        cols = slice(h * DV, (h + 1) * DV)
        heads.append(_rms(o_ref[:, cols] + ob_ref[0, :, cols], ghead_ref[...]))
    o_gla = jnp.concatenate(heads, axis=1) * _silu(p_ref[:, M_OG:M_OG + GLA_W])

    u = p_ref[:, M_SC:M_SC + SC_W] * p_ref[:, M_SX:M_SX + SC_W]
    pos = lax.broadcasted_iota(jnp.int32, (n, 1), 0) % GRID_W
    left = jnp.where(pos == 0, 0.0, pltpu.roll(u, 1, 0))
    right = jnp.where(pos == GRID_W - 1, 0.0, pltpu.roll(u, n - 1, 0))
    conv = wsc_ref[0:1, :] * left + wsc_ref[1:2, :] * u + wsc_ref[2:3, :] * right + bsc_ref[...]
    o_sc = p_ref[:, M_SB:M_SB + SC_W] * conv

    yx = jnp.concatenate([o_gla, o_sc], axis=1).astype(BF16)
    x1_ref[0] = x + _rms(_dot(yx, wout_ref[...]), gpost_ref[...] * gt_ref[0])


def _mixer(x, o_b, kqv, a_low, hx, s_f, mod3, g_post, w_fwd, wg_f, bg_f, g_head, w_sc, b_sc,
           w_out):
    bsz, t, d = x.shape
    nt = t // TILE_SCAN
    const = lambda shape: pl.BlockSpec(shape, lambda b, i: (0,) * len(shape),
                                       pipeline_mode=pl.Buffered(1))
    mod = lambda j: pl.BlockSpec((1, 1, d), lambda b, i: (b * 6 + j, 0, 0))
    return pl.pallas_call(
        _mixer_kernel,
        grid=(bsz, nt),
        in_specs=[
            pl.BlockSpec((1, TILE_SCAN, d), lambda b, i: (b, i, 0)),
            pl.BlockSpec((1, TILE_SCAN, GLA_W), lambda b, i: (b, i, 0)),
            pl.BlockSpec((1, TILE_SCAN, B_A), lambda b, i: (b, i, 0)),
            pl.BlockSpec((1, TILE_SCAN, LANES), lambda b, i: (b, i, 0)),
            pl.BlockSpec((1, TILE_SCAN, d), lambda b, i: (b, i, 0)),
            pl.BlockSpec((1,) + STATE_SHAPE, lambda b, i: (b, 0, 0, 0)),
            mod(2),
            const((1, d)),
            const((d, M_COLS)),
            const((LANES, QK)), const((1, QK)),
            const((1, DV)),
            const((3, SC_W)), const((1, SC_W)),
            const((d, d)),
        ],
        out_specs=pl.BlockSpec((1, TILE_SCAN, d), lambda b, i: (b, i, 0)),
        out_shape=jax.ShapeDtypeStruct((bsz, t, d), F32),
        scratch_shapes=[pltpu.VMEM((TILE_SCAN, M_COLS), F32), pltpu.VMEM((TILE_SCAN, GLA_W), F32),
                        pltpu.VMEM(STATE_SHAPE, F32)],
        compiler_params=pltpu.CompilerParams(
            dimension_semantics=("arbitrary", "arbitrary"), vmem_limit_bytes=VMEM_LIMIT),
        name="mixer",
    )(x, o_b, kqv, a_low, hx, s_f, mod3, g_post, w_fwd, wg_f, bg_f, g_head, w_sc, b_sc, w_out)


def _ffn_kernel(xm_ref, xp_ref, xn_ref, sh_ref, sc_ref, gt_ref, gpre_ref, gpost_ref,
                wup_ref, wcf_ref, bcf_ref, wdown_ref, out_ref, hx_ref, h_ref):
    i = pl.program_id(1)
    n = xm_ref.shape[1]
    ext = n + 2 * GRID_W

    def prep(v):
        return (_rms(v, gpre_ref[...] * (1.0 + sc_ref[0])) + sh_ref[0]).astype(BF16)

    x1 = xm_ref[0]
    hx_ref[0:GRID_W, :] = prep(xp_ref[0])
    hx_ref[GRID_W:GRID_W + n, :] = prep(x1)
    hx_ref[GRID_W + n:ext, :] = prep(xn_ref[0])

    rowid = lax.broadcasted_iota(jnp.int32, (ext, 1), 0)
    has_prev = (i > 0).astype(F32)
    has_next = (i < pl.num_programs(1) - 1).astype(F32)
    pos = rowid % GRID_W

    for c in range(D_FF // MXU_N):
        cs = slice(c * MXU_N, (c + 1) * MXU_N)
        gs = slice(D_FF + c * MXU_N, D_FF + (c + 1) * MXU_N)
        u = _dot(hx_ref[...], wup_ref[:, cs])
        u = jnp.concatenate([u[0:GRID_W] * has_prev, u[GRID_W:GRID_W + n],
                             u[GRID_W + n:ext] * has_next], axis=0)
        gate = _dot(hx_ref[GRID_W:GRID_W + n, :], wup_ref[:, gs])
        u_l = jnp.where(pos == 0, 0.0, pltpu.roll(u, 1, 0)).astype(BF16)
        u_r = jnp.where(pos == GRID_W - 1, 0.0, pltpu.roll(u, ext - 1, 0)).astype(BF16)
        u_c = u.astype(BF16)
        w9 = wcf_ref[:, cs].astype(BF16)
        y = bcf_ref[:, cs].astype(BF16)
        for dr in range(3):
            rs = slice(dr * GRID_W, dr * GRID_W + n)
            y = (y + w9[3 * dr:3 * dr + 1] * u_l[rs]
                 + w9[3 * dr + 1:3 * dr + 2] * u_c[rs]
                 + w9[3 * dr + 2:3 * dr + 3] * u_r[rs])
        h_ref[:, cs] = (_silu(y.astype(F32)) * gate).astype(BF16)

    z = (_dot(h_ref[:, :FFN_DOWN_SPLIT], wdown_ref[:FFN_DOWN_SPLIT, :])
         + _dot(h_ref[:, FFN_DOWN_SPLIT:], wdown_ref[FFN_DOWN_SPLIT:, :]))
    out_ref[0] = x1 + _rms(z, gpost_ref[...] * gt_ref[0])


def _ffn(x1, mod3, g_pre, g_post, w_up, w_cf, b_cf, w_down):
    bsz, t, d = x1.shape
    nt = t // TILE_FFN
    rows_per_tile = TILE_FFN // GRID_W
    n_rows = t // GRID_W
    const = lambda shape: pl.BlockSpec(shape, lambda b, i: (0,) * len(shape),
                                       pipeline_mode=pl.Buffered(1))
    mod = lambda j: pl.BlockSpec((1, 1, d), lambda b, i: (b * 6 + j, 0, 0))
    return pl.pallas_call(
        _ffn_kernel,
        grid=(bsz, nt),
        in_specs=[
            pl.BlockSpec((1, TILE_FFN, d), lambda b, i: (b, i, 0)),
            pl.BlockSpec((1, GRID_W, d),
                         lambda b, i: (b, jnp.maximum(i * rows_per_tile - 1, 0), 0)),
            pl.BlockSpec((1, GRID_W, d),
                         lambda b, i: (b, jnp.minimum((i + 1) * rows_per_tile, n_rows - 1), 0)),
            mod(3), mod(4), mod(5),
            const((1, d)), const((1, d)),
            const((d, 2 * D_FF)),
            const((9, D_FF)), const((1, D_FF)),
            const((D_FF, d)),
        ],
        out_specs=pl.BlockSpec((1, TILE_FFN, d), lambda b, i: (b, i, 0)),
        out_shape=jax.ShapeDtypeStruct((bsz, t, d), F32),
        scratch_shapes=[pltpu.VMEM((TILE_FFN + 2 * GRID_W, d), BF16),
                        pltpu.VMEM((TILE_FFN, D_FF), BF16)],
        compiler_params=pltpu.CompilerParams(
            dimension_semantics=("arbitrary", "arbitrary"), vmem_limit_bytes=VMEM_LIMIT),
        name="ffn",
    )(x1, x1, x1, mod3, mod3, mod3, g_pre, g_post, w_up, w_cf, b_cf, w_down)


def kernel(x, c, ctx, c_ctx, w_ada, b_ada, g_pre_mix, g_post_mix, g_pre_ffn, g_post_ffn,
           w_in, w_af, b_af, w_ab, b_ab, g_head, w_sc, b_sc, w_out, w_up, w_cf, b_cf, w_down):
    bsz, t, d = x.shape
    assert d == D_MODEL and t % TILE_SCAN == 0 and t % TILE_FFN == 0 and w_in.shape[0] == 1

    cond = jnp.concatenate([c, c_ctx[None, :], jnp.zeros((16 - bsz - 1, d), F32)], axis=0)
    mod = _adaln(cond, w_ada[0], b_ada[0][None, :])
    mod3 = mod.reshape(16 * 6, 1, d)

    wi = w_in[0]
    sl = lambda a, b: wi[:, a:b]
    gate_cols = jnp.concatenate(
        [sl(COL_AF, COL_Q), jnp.zeros((d, GATE_PAD - 2 * LOWRANK), F32)], axis=1)
    w_fwd = sl(COL_OG, D_IN).astype(BF16)
    w_bwd = jnp.concatenate(
        [sl(COL_K, COL_V), sl(COL_Q, COL_OG), sl(COL_V, COL_AF), gate_cols], axis=1).astype(BF16)
    w_ctx = jnp.concatenate([sl(COL_K, COL_V), sl(COL_V, COL_AF), gate_cols], axis=1).astype(BF16)
    wg = jnp.zeros((LANES, 2 * QK), F32)
    wg = wg.at[0:LOWRANK, 0:QK].set(w_af[0]).at[LOWRANK:2 * LOWRANK, QK:].set(w_ab[0]).astype(BF16)
    bg = jnp.concatenate([b_af[0], b_ab[0]])[None, :]

    s_f, s_b = _ctx_states(ctx, mod3, g_pre_mix, w_ctx, wg, bg)
    o_b, kqv, a_low, hx = _gla_bwd(x, s_b, mod3, g_pre_mix, w_bwd, wg[:, QK:], bg[:, QK:])
    x1 = _mixer(x, o_b, kqv, a_low, hx, s_f, mod3, g_post_mix, w_fwd, wg[:, :QK],
                bg[:, :QK], g_head, w_sc[0], b_sc, w_out[0].astype(BF16))
    return _ffn(x1, mod3, g_pre_ffn, g_post_ffn, w_up[0].astype(BF16),
                w_cf[0].reshape(9, D_FF), b_cf, w_down[0].astype(BF16))
```

```python
import functools

import jax
import jax.numpy as jnp
from jax import lax
from jax.experimental import pallas as pl
from jax.experimental.pallas import tpu as pltpu

F32 = jnp.float32
BF16 = jnp.bfloat16

D_MODEL = 1024
GRID_W = 64
HEADS = 4
DK = 64
DV = 128
QK = HEADS * DK
GLA_W = HEADS * DV
LOWRANK = 16
TAU = 16.0
CHUNK = 64
SC_W = 512
D_FF = 2816
EPS = 1e-6

COL_K = 0
COL_V = COL_K + QK
COL_AF = COL_V + GLA_W
COL_AB = COL_AF + LOWRANK
COL_Q = COL_AB + LOWRANK
COL_OG = COL_Q + QK
COL_SB = COL_OG + GLA_W
COL_SC = COL_SB + SC_W
COL_SX = COL_SC + SC_W
D_IN = COL_SX + SC_W

LANES = 128
PAIRS = HEADS // 2
PAIR_K = 2 * DK
PAIR_V = 2 * DV
STATE_SHAPE = (PAIRS, PAIR_K, PAIR_V)
MXU_N = 256
FFN_DOWN_SPLIT = 8 * MXU_N
TILE_SCAN = 1024
TILE_FFN = 1024
VMEM_LIMIT = 56 * 1024 * 1024

M_OG, M_SB, M_SC, M_SX = 0, 512, 1024, 1536
M_COLS = M_SX + SC_W
GATE_PAD = LANES
B_K, B_Q, B_V, B_A = 0, 256, 512, 1024
B_COLS = B_A + GATE_PAD
C_K, C_V, C_A, C_COLS = B_K, B_V, B_A, B_COLS
W_MIX_AT = M_COLS
W_ALL_COLS = W_MIX_AT + M_COLS


def _dot(a, b):
    return jnp.dot(a, b, preferred_element_type=F32)


def _dot_tn(a, b):
    return lax.dot_general(a, b, (((0,), (0,)), ((), ())), preferred_element_type=F32)


def _dot_nt(a, b):
    return lax.dot_general(a, b, (((1,), (1,)), ((), ())), preferred_element_type=F32)


def _split_bf16(x):
    hi = x.astype(BF16)
    lo = (x - hi.astype(F32)).astype(BF16)
    return hi, lo


def _rms(x, g):
    ms = jnp.mean(x * x, axis=-1, keepdims=True)
    return x * lax.rsqrt(ms + EPS) * g


def _silu(x):
    return x * jax.nn.sigmoid(x)


def _log_sigmoid(a):
    return jnp.minimum(a, 0.0) - jnp.log(1.0 + jnp.exp(-jnp.abs(a)))


def _adaln_kernel(c_ref, w_ref, b_ref, o_ref):
    s = _silu(c_ref[...])
    s_hi, s_lo = _split_bf16(s)
    w_hi, w_lo = _split_bf16(w_ref[...])
    o_ref[...] = _dot(s_hi, w_hi) + _dot(s_lo, w_hi) + _dot(s_hi, w_lo) + b_ref[...]


def _adaln(cond, w, b):
    rows, d = cond.shape
    n = w.shape[1]
    bn = 1024
    return pl.pallas_call(
        _adaln_kernel,
        grid=(n // bn,),
        in_specs=[
            pl.BlockSpec((rows, d), lambda j: (0, 0)),
            pl.BlockSpec((d, bn), lambda j: (0, j)),
            pl.BlockSpec((1, bn), lambda j: (0, j)),
        ],
        out_specs=pl.BlockSpec((rows, bn), lambda j: (0, j)),
        out_shape=jax.ShapeDtypeStruct((rows, n), F32),
        compiler_params=pltpu.CompilerParams(
            dimension_semantics=("arbitrary",), vmem_limit_bytes=VMEM_LIMIT),
        name="adaln",
    )(cond, w, b)


def _pair_block_mask():
    r = lax.broadcasted_iota(jnp.int32, (PAIR_K, PAIR_V), 0) // DK
    c = lax.broadcasted_iota(jnp.int32, (PAIR_K, PAIR_V), 1) // DV
    return r == c


def _pair_states(full, mask):
    return [jnp.where(mask, full[p * PAIR_K:(p + 1) * PAIR_K, p * PAIR_V:(p + 1) * PAIR_V], 0.0)
            for p in range(PAIRS)]


def _ctx_kernel(ctx_ref, sh_ref, sc_ref, gpre_ref, w_ref, wg_ref, bg_ref, sf_ref, sb_ref):
    n = ctx_ref.shape[1]
    hc = _rms(ctx_ref[0], gpre_ref[...] * (1.0 + sc_ref[0])) + sh_ref[0]
    pc = _dot(hc.astype(BF16), w_ref[...])
    k = pc[:, C_K:C_K + QK]
    v = pc[:, C_V:C_V + GLA_W].astype(BF16)
    a = _dot(pc[:, C_A:C_A + LANES].astype(BF16), wg_ref[...]) + bg_ref[...]
    g = _log_sigmoid(a) * (1.0 / TAU)
    row = lax.broadcasted_iota(jnp.int32, (n, n), 0)
    col = lax.broadcasted_iota(jnp.int32, (n, n), 1)
    later = (col > row).astype(BF16)
    earlier = (col < row).astype(BF16)
    gf_hi, gf_lo = _split_bf16(g[:, :QK])
    gb_hi, gb_lo = _split_bf16(g[:, QK:])
    wf = jnp.exp(_dot(later, gf_hi) + _dot(later, gf_lo))
    wb = jnp.exp(_dot(earlier, gb_hi) + _dot(earlier, gb_lo))
    mask = _pair_block_mask()
    for p, s in enumerate(_pair_states(_dot_tn((k * wf).astype(BF16), v), mask)):
        sf_ref[0, p] = s
    for p, s in enumerate(_pair_states(_dot_tn((k * wb).astype(BF16), v), mask)):
        sb_ref[0, p] = s


def _ctx_states(ctx, mod3, g_pre, w_ctx, wg, bg):
    bsz, n, d = ctx.shape
    ctx_row = bsz * 6
    const = lambda shape: pl.BlockSpec(shape, lambda b: (0,) * len(shape))
    state = jax.ShapeDtypeStruct((bsz,) + STATE_SHAPE, F32)
    return pl.pallas_call(
        _ctx_kernel,
        grid=(bsz,),
        in_specs=[
            pl.BlockSpec((1, n, d), lambda b: (b, 0, 0)),
            pl.BlockSpec((1, 1, d), lambda b: (ctx_row + 0, 0, 0)),
            pl.BlockSpec((1, 1, d), lambda b: (ctx_row + 1, 0, 0)),
            const((1, d)),
            const((d, C_COLS)),
            const((LANES, 2 * QK)),
            const((1, 2 * QK)),
        ],
        out_specs=[pl.BlockSpec((1,) + STATE_SHAPE, lambda b: (b, 0, 0, 0))] * 2,
        out_shape=[state, state],
        compiler_params=pltpu.CompilerParams(
            dimension_semantics=("arbitrary",), vmem_limit_bytes=VMEM_LIMIT),
        name="ctx_state",
    )(ctx, mod3, mod3, g_pre, w_ctx, wg, bg)


def _gla_tile(p_ref, cols, g, state_ref, o_ref, reverse):
    col_k, col_q, col_v = cols
    n_chunks = p_ref.shape[0] // CHUNK
    ii = lax.broadcasted_iota(jnp.int32, (CHUNK, CHUNK), 0)
    jj = lax.broadcasted_iota(jnp.int32, (CHUNK, CHUNK), 1)
    i_all = lax.broadcasted_iota(jnp.int32, (CHUNK, HEADS * CHUNK), 0)
    j_all = lax.broadcasted_iota(jnp.int32, (CHUNK, HEADS * CHUNK), 1) % CHUNK
    if reverse:
        tri = (jj >= ii).astype(BF16)
        keep = j_all > i_all
        ref_row, last_row = CHUNK // 2, 0
    else:
        tri = (jj <= ii).astype(BF16)
        keep = j_all <= i_all
        ref_row, last_row = CHUNK // 2 - 1, CHUNK - 1
    tri2 = jnp.concatenate([tri, tri], axis=1)
    own_head = (lax.broadcasted_iota(jnp.int32, (HEADS * CHUNK, QK), 0) // CHUNK
                == lax.broadcasted_iota(jnp.int32, (HEADS * CHUNK, QK), 1) // DK)
    block = _pair_block_mask()

    chunks = range(n_chunks)
    order = range(n_chunks - 1, -1, -1) if reverse else range(n_chunks)
    rows = [slice(c * CHUNK, (c + 1) * CHUNK) for c in chunks]
    pair_k = [slice(p * PAIR_K, (p + 1) * PAIR_K) for p in range(PAIRS)]
    pair_v = [slice(p * PAIR_V, (p + 1) * PAIR_V) for p in range(PAIRS)]

    bs = []
    for c in chunks:
        g_hi, g_lo = _split_bf16(g[rows[c], :])
        bs.append(_dot(tri2, jnp.concatenate([g_hi, g_lo], axis=0)))

    q_ts, k_stacks, q_hs, k_hs, decays = [], [], [], [], []
    for c in chunks:
        b = bs[c]
        b_ref = b[ref_row:ref_row + 1, :]
        b_last = b[last_row:last_row + 1, :]
        e = b - b_ref
        q_t = p_ref[rows[c], col_q:col_q + QK] * (DK ** -0.5) * jnp.exp(e)
        k_t = p_ref[rows[c], col_k:col_k + QK] * jnp.exp(-e)
        q_ts.append(q_t.astype(BF16))
        k_stacks.append(
            jnp.where(own_head, jnp.concatenate([k_t] * HEADS, axis=0), 0.0).astype(BF16))
        q_hs.append((q_t * jnp.exp(b_ref)).astype(BF16))
        k_hs.append((k_t * jnp.exp(b_last - b_ref)).astype(BF16))
        decays.append(jnp.exp(b_last))

    atts = [jnp.where(keep, _dot_nt(q_ts[c], k_stacks[c]), 0.0).astype(BF16) for c in chunks]

    upds, decay_rows, v_diags = [], [], []
    for c in chunks:
        u_c, d_c, v_c = [], [], []
        for p in range(PAIRS):
            v_pair = p_ref[rows[c], col_v + p * PAIR_V:col_v + (p + 1) * PAIR_V].astype(F32)
            v_c.append(jnp.where(block, jnp.concatenate([v_pair, v_pair], axis=0), 0.0).astype(BF16))
            u_c.append(jnp.where(block, _dot_tn(k_hs[c][:, pair_k[p]], v_pair.astype(BF16)), 0.0))
            d_c.append(jnp.broadcast_to(decays[c][:, pair_k[p]], (PAIR_V, PAIR_K)).T)
        upds.append(u_c)
        decay_rows.append(d_c)
        v_diags.append(v_c)

    seen = [[None] * PAIRS for _ in chunks]
    for p in range(PAIRS):
        s = state_ref[p]
        for c in order:
            seen[c][p] = s.astype(BF16)
            s = s * decay_rows[c][p] + upds[c][p]
        state_ref[p] = s

    for c in chunks:
        for p in range(PAIRS):
            lhs = jnp.concatenate([atts[c][:, pair_k[p]], q_hs[c][:, pair_k[p]]], axis=1)
            rhs = jnp.concatenate([v_diags[c][p], seen[c][p]], axis=0)
            o_ref[rows[c], pair_v[p]] = _dot(lhs, rhs)


def _gate(a_low, wg_ref, bg_ref):
    a = _dot(a_low.astype(BF16), wg_ref[...]) + bg_ref[...]
    return _log_sigmoid(a) * (1.0 / TAU)


def _gla_bwd_kernel(x_ref, s0_ref, sh_ref, sc_ref, gpre_ref, w_ref, wg_ref, bg_ref,
                    ob_ref, kqv_ref, alow_ref, hx_ref, p_ref, state_ref):
    @pl.when(pl.program_id(1) == 0)
    def _():
        state_ref[...] = s0_ref[0]

    hx = (_rms(x_ref[0], gpre_ref[...] * (1.0 + sc_ref[0])) + sh_ref[0]).astype(BF16)
    hx_ref[0] = hx
    half = hx.shape[0] // 2
    p_ref[:, 0:B_A] = _dot(hx, w_ref[:, 0:B_A])
    p_ref[0:half, B_A:B_COLS] = _dot(hx[0:half], w_ref[:, B_A:B_COLS])
    p_ref[half:, B_A:B_COLS] = _dot(hx[half:], w_ref[:, B_A:B_COLS])
    kqv_ref[0] = p_ref[:, 0:B_A].astype(BF16)
    alow_ref[0] = p_ref[:, B_A:B_A + LANES]
    g = _gate(p_ref[:, B_A:B_A + LANES], wg_ref, bg_ref)
    _gla_tile(p_ref, (B_K, B_Q, B_V), g, state_ref, ob_ref.at[0], reverse=True)


def _gla_bwd(x, s_b, mod3, g_pre, w_bwd, wg_b, bg_b):
    bsz, t, d = x.shape
    nt = t // TILE_SCAN
    const = lambda shape: pl.BlockSpec(shape, lambda b, i: (0,) * len(shape),
                                       pipeline_mode=pl.Buffered(1))
    return pl.pallas_call(
        _gla_bwd_kernel,
        grid=(bsz, nt),
        in_specs=[
            pl.BlockSpec((1, TILE_SCAN, d), lambda b, i: (b, nt - 1 - i, 0)),
            pl.BlockSpec((1,) + STATE_SHAPE, lambda b, i: (b, 0, 0, 0)),
            pl.BlockSpec((1, 1, d), lambda b, i: (b * 6 + 0, 0, 0)),
            pl.BlockSpec((1, 1, d), lambda b, i: (b * 6 + 1, 0, 0)),
            const((1, d)),
            const((d, B_COLS)),
            const((LANES, QK)),
            const((1, QK)),
        ],
        out_specs=[pl.BlockSpec((1, TILE_SCAN, GLA_W), lambda b, i: (b, nt - 1 - i, 0)),
                   pl.BlockSpec((1, TILE_SCAN, B_A), lambda b, i: (b, nt - 1 - i, 0)),
                   pl.BlockSpec((1, TILE_SCAN, LANES), lambda b, i: (b, nt - 1 - i, 0)),
                   pl.BlockSpec((1, TILE_SCAN, d), lambda b, i: (b, nt - 1 - i, 0))],
        out_shape=[jax.ShapeDtypeStruct((bsz, t, GLA_W), F32),
                   jax.ShapeDtypeStruct((bsz, t, B_A), BF16),
                   jax.ShapeDtypeStruct((bsz, t, LANES), F32),
                   jax.ShapeDtypeStruct((bsz, t, d), BF16)],
        scratch_shapes=[pltpu.VMEM((TILE_SCAN, B_COLS), F32), pltpu.VMEM(STATE_SHAPE, F32)],
        compiler_params=pltpu.CompilerParams(
            dimension_semantics=("arbitrary", "arbitrary"), vmem_limit_bytes=VMEM_LIMIT),
        name="gla_bwd",
    )(x, s_b, mod3, mod3, g_pre, w_bwd, wg_b, bg_b)


def _mixer_kernel(x_ref, ob_ref, kqv_ref, alow_ref, hx_ref, s0_ref, gt_ref,
                  gpost_ref, w_ref, wg_ref, bg_ref, ghead_ref, wsc_ref, bsc_ref, wout_ref,
                  x1_ref, p_ref, o_ref, state_ref):
    @pl.when(pl.program_id(1) == 0)
    def _():
        state_ref[...] = s0_ref[0]

    n = x_ref.shape[1]
    x = x_ref[0]
    p_ref[...] = _dot(hx_ref[0], w_ref[...])
    g = _gate(alow_ref[0], wg_ref, bg_ref)
    _gla_tile(kqv_ref.at[0], (B_K, B_Q, B_V), g, state_ref, o_ref, reverse=False)

    heads = []
    for h in range(HEADS):
        cols = slice(h * DV, (h + 1) * DV)
        heads.append(_rms(o_ref[:, cols] + ob_ref[0, :, cols], ghead_ref[...]))
    o_gla = jnp.concatenate(heads, axis=1) * _silu(p_ref[:, M_OG:M_OG + GLA_W])

    u = p_ref[:, M_SC:M_SC + SC_W] * p_ref[:, M_SX:M_SX + SC_W]
    pos = lax.broadcasted_iota(jnp.int32, (n, 1), 0) % GRID_W
    left = jnp.where(pos == 0, 0.0, pltpu.roll(u, 1, 0))
    right = jnp.where(pos == GRID_W - 1, 0.0, pltpu.roll(u, n - 1, 0))
    conv = wsc_ref[0:1, :] * left + wsc_ref[1:2, :] * u + wsc_ref[2:3, :] * right + bsc_ref[...]
    o_sc = p_ref[:, M_SB:M_SB + SC_W] * conv

    yx = jnp.concatenate([o_gla, o_sc], axis=1).astype(BF16)
    x1_ref[0] = x + _rms(_dot(yx, wout_ref[...]), gpost_ref[...] * gt_ref[0])


def _mixer(x, o_b, kqv, a_low, hx, s_f, mod3, g_post, w_fwd, wg_f, bg_f, g_head, w_sc, b_sc,
           w_out):
    bsz, t, d = x.shape
    nt = t // TILE_SCAN
    const = lambda shape: pl.BlockSpec(shape, lambda b, i: (0,) * len(shape),
                                       pipeline_mode=pl.Buffered(1))
    mod = lambda j: pl.BlockSpec((1, 1, d), lambda b, i: (b * 6 + j, 0, 0))
    return pl.pallas_call(
        _mixer_kernel,
        grid=(bsz, nt),
        in_specs=[
            pl.BlockSpec((1, TILE_SCAN, d), lambda b, i: (b, i, 0)),
            pl.BlockSpec((1, TILE_SCAN, GLA_W), lambda b, i: (b, i, 0)),
            pl.BlockSpec((1, TILE_SCAN, B_A), lambda b, i: (b, i, 0)),
            pl.BlockSpec((1, TILE_SCAN, LANES), lambda b, i: (b, i, 0)),
            pl.BlockSpec((1, TILE_SCAN, d), lambda b, i: (b, i, 0)),
            pl.BlockSpec((1,) + STATE_SHAPE, lambda b, i: (b, 0, 0, 0)),
            mod(2),
            const((1, d)),
            pl.BlockSpec((d, M_COLS), lambda b, i: (0, W_MIX_AT // M_COLS),
                         pipeline_mode=pl.Buffered(1)),
            const((LANES, QK)), const((1, QK)),
            const((1, DV)),
            const((3, SC_W)), const((1, SC_W)),
            const((d, d)),
        ],
        out_specs=pl.BlockSpec((1, TILE_SCAN, d), lambda b, i: (b, i, 0)),
        out_shape=jax.ShapeDtypeStruct((bsz, t, d), F32),
        scratch_shapes=[pltpu.VMEM((TILE_SCAN, M_COLS), F32), pltpu.VMEM((TILE_SCAN, GLA_W), F32),
                        pltpu.VMEM(STATE_SHAPE, F32)],
        compiler_params=pltpu.CompilerParams(
            dimension_semantics=("arbitrary", "arbitrary"), vmem_limit_bytes=VMEM_LIMIT),
        name="mixer",
    )(x, o_b, kqv, a_low, hx, s_f, mod3, g_post, w_fwd, wg_f, bg_f, g_head, w_sc, b_sc, w_out)


def _ffn_kernel(xm_ref, xp_ref, xn_ref, sh_ref, sc_ref, gt_ref, gpre_ref, gpost_ref,
                wup_ref, wcf_ref, bcf_ref, wdown_ref, out_ref, hx_ref, h_ref):
    i = pl.program_id(1)
    n = xm_ref.shape[1]
    ext = n + 2 * GRID_W

    def prep(v):
        return (_rms(v, gpre_ref[...] * (1.0 + sc_ref[0])) + sh_ref[0]).astype(BF16)

    x1 = xm_ref[0]
    hx_ref[0:GRID_W, :] = prep(xp_ref[0])
    hx_ref[GRID_W:GRID_W + n, :] = prep(x1)
    hx_ref[GRID_W + n:ext, :] = prep(xn_ref[0])

    rowid = lax.broadcasted_iota(jnp.int32, (ext, 1), 0)
    has_prev = (i > 0).astype(F32)
    has_next = (i < pl.num_programs(1) - 1).astype(F32)
    pos = rowid % GRID_W

    for c in range(D_FF // MXU_N):
        cs = slice(c * MXU_N, (c + 1) * MXU_N)
        gs = slice(D_FF + c * MXU_N, D_FF + (c + 1) * MXU_N)
        u = _dot(hx_ref[...], wup_ref[:, cs])
        u = jnp.concatenate([u[0:GRID_W] * has_prev, u[GRID_W:GRID_W + n],
                             u[GRID_W + n:ext] * has_next], axis=0)
        gate = _dot(hx_ref[GRID_W:GRID_W + n, :], wup_ref[:, gs])
        u_l = jnp.where(pos == 0, 0.0, pltpu.roll(u, 1, 0)).astype(BF16)
        u_r = jnp.where(pos == GRID_W - 1, 0.0, pltpu.roll(u, ext - 1, 0)).astype(BF16)
        u_c = u.astype(BF16)
        w9 = wcf_ref[:, cs].astype(BF16)
        y = bcf_ref[:, cs].astype(BF16)
        for dr in range(3):
            rs = slice(dr * GRID_W, dr * GRID_W + n)
            y = (y + w9[3 * dr:3 * dr + 1] * u_l[rs]
                 + w9[3 * dr + 1:3 * dr + 2] * u_c[rs]
                 + w9[3 * dr + 2:3 * dr + 3] * u_r[rs])
        h_ref[:, cs] = (_silu(y.astype(F32)) * gate).astype(BF16)

    z = (_dot(h_ref[:, :FFN_DOWN_SPLIT], wdown_ref[:FFN_DOWN_SPLIT, :])
         + _dot(h_ref[:, FFN_DOWN_SPLIT:], wdown_ref[FFN_DOWN_SPLIT:, :]))
    out_ref[0] = x1 + _rms(z, gpost_ref[...] * gt_ref[0])


def _ffn(x1, mod3, g_pre, g_post, w_up, w_cf, b_cf, w_down):
    bsz, t, d = x1.shape
    nt = t // TILE_FFN
    rows_per_tile = TILE_FFN // GRID_W
    n_rows = t // GRID_W
    const = lambda shape: pl.BlockSpec(shape, lambda b, i: (0,) * len(shape),
                                       pipeline_mode=pl.Buffered(1))
    mod = lambda j: pl.BlockSpec((1, 1, d), lambda b, i: (b * 6 + j, 0, 0))
    return pl.pallas_call(
        _ffn_kernel,
        grid=(bsz, nt),
        in_specs=[
            pl.BlockSpec((1, TILE_FFN, d), lambda b, i: (b, i, 0)),
            pl.BlockSpec((1, GRID_W, d),
                         lambda b, i: (b, jnp.maximum(i * rows_per_tile - 1, 0), 0)),
            pl.BlockSpec((1, GRID_W, d),
                         lambda b, i: (b, jnp.minimum((i + 1) * rows_per_tile, n_rows - 1), 0)),
            mod(3), mod(4), mod(5),
            const((1, d)), const((1, d)),
            const((d, 2 * D_FF)),
            const((9, D_FF)), const((1, D_FF)),
            const((D_FF, d)),
        ],
        out_specs=pl.BlockSpec((1, TILE_FFN, d), lambda b, i: (b, i, 0)),
        out_shape=jax.ShapeDtypeStruct((bsz, t, d), F32),
        scratch_shapes=[pltpu.VMEM((TILE_FFN + 2 * GRID_W, d), BF16),
                        pltpu.VMEM((TILE_FFN, D_FF), BF16)],
        compiler_params=pltpu.CompilerParams(
            dimension_semantics=("arbitrary", "arbitrary"), vmem_limit_bytes=VMEM_LIMIT),
        name="ffn",
    )(x1, x1, x1, mod3, mod3, mod3, g_pre, g_post, w_up, w_cf, b_cf, w_down)


def kernel(x, c, ctx, c_ctx, w_ada, b_ada, g_pre_mix, g_post_mix, g_pre_ffn, g_post_ffn,
           w_in, w_af, b_af, w_ab, b_ab, g_head, w_sc, b_sc, w_out, w_up, w_cf, b_cf, w_down):
    bsz, t, d = x.shape
    assert d == D_MODEL and t % TILE_SCAN == 0 and t % TILE_FFN == 0 and w_in.shape[0] == 1

    cond = jnp.concatenate([c, c_ctx[None, :], jnp.zeros((16 - bsz - 1, d), F32)], axis=0)
    mod = _adaln(cond, w_ada[0], b_ada[0][None, :])
    mod3 = mod.reshape(16 * 6, 1, d)

    wi = w_in[0]
    sl = lambda a, b: wi[:, a:b]
    w_all = jnp.concatenate(
        [sl(COL_K, COL_V), sl(COL_Q, COL_OG), sl(COL_V, COL_AF), sl(COL_AF, COL_Q),
         jnp.zeros((d, W_MIX_AT - B_A - 2 * LOWRANK), F32), sl(COL_OG, D_IN)],
        axis=1).astype(BF16)
    wg = jnp.zeros((LANES, 2 * QK), F32)
    wg = wg.at[0:LOWRANK, 0:QK].set(w_af[0]).at[LOWRANK:2 * LOWRANK, QK:].set(w_ab[0]).astype(BF16)
    bg = jnp.concatenate([b_af[0], b_ab[0]])[None, :]

    s_f, s_b = _ctx_states(ctx, mod3, g_pre_mix, w_all, wg, bg)
    o_b, kqv, a_low, hx = _gla_bwd(x, s_b, mod3, g_pre_mix, w_all, wg[:, QK:], bg[:, QK:])
    x1 = _mixer(x, o_b, kqv, a_low, hx, s_f, mod3, g_post_mix, w_all, wg[:, :QK],
                bg[:, :QK], g_head, w_sc[0], b_sc, w_out[0].astype(BF16))
    return _ffn(x1, mod3, g_pre_ffn, g_post_ffn, w_up[0].astype(BF16),
                w_cf[0].reshape(9, D_FF), b_cf, w_down[0].astype(BF16))
```

```python
import functools

import jax
import jax.numpy as jnp
from jax import lax
from jax.experimental import pallas as pl
from jax.experimental.pallas import tpu as pltpu

F32 = jnp.float32
BF16 = jnp.bfloat16

D_MODEL = 1024
GRID_W = 64
HEADS = 4
DK = 64
DV = 128
QK = HEADS * DK
GLA_W = HEADS * DV
LOWRANK = 16
TAU = 16.0
CHUNK = 64
SC_W = 512
D_FF = 2816
EPS = 1e-6

COL_K = 0
COL_V = COL_K + QK
COL_AF = COL_V + GLA_W
COL_AB = COL_AF + LOWRANK
COL_Q = COL_AB + LOWRANK
COL_OG = COL_Q + QK
COL_SB = COL_OG + GLA_W
COL_SC = COL_SB + SC_W
COL_SX = COL_SC + SC_W
D_IN = COL_SX + SC_W

LANES = 128
PAIRS = HEADS // 2
PAIR_K = 2 * DK
PAIR_V = 2 * DV
STATE_SHAPE = (PAIRS, PAIR_K, PAIR_V)
MXU_N = 256
TILE_SCAN = 1024
TILE_FFN = 1024
VMEM_LIMIT = 56 * 1024 * 1024

M_OG, M_SB, M_SC, M_SX = 0, 512, 1024, 1536
M_COLS = M_SX + SC_W
GATE_PAD = LANES
B_K, B_Q, B_V, B_A = 0, 256, 512, 1024
B_COLS = B_A + GATE_PAD
C_K, C_V, C_A = 0, 256, 768
C_COLS = C_A + GATE_PAD


def _dot(a, b):
    return jnp.dot(a, b, preferred_element_type=F32)


def _dot_tn(a, b):
    return lax.dot_general(a, b, (((0,), (0,)), ((), ())), preferred_element_type=F32)


def _dot_nt(a, b):
    return lax.dot_general(a, b, (((1,), (1,)), ((), ())), preferred_element_type=F32)


def _split_bf16(x):
    hi = x.astype(BF16)
    lo = (x - hi.astype(F32)).astype(BF16)
    return hi, lo


def _rms(x, g):
    ms = jnp.mean(x * x, axis=-1, keepdims=True)
    return x * lax.rsqrt(ms + EPS) * g


def _silu(x):
    return x * jax.nn.sigmoid(x)


def _log_sigmoid(a):
    return jnp.minimum(a, 0.0) - jnp.log(1.0 + jnp.exp(-jnp.abs(a)))


def _adaln_kernel(c_ref, w_ref, b_ref, o_ref):
    s = _silu(c_ref[...])
    s_hi, s_lo = _split_bf16(s)
    w_hi, w_lo = _split_bf16(w_ref[...])
    o_ref[...] = _dot(s_hi, w_hi) + _dot(s_lo, w_hi) + _dot(s_hi, w_lo) + b_ref[...]


def _adaln(cond, w, b):
    rows, d = cond.shape
    n = w.shape[1]
    bn = 1024
    return pl.pallas_call(
        _adaln_kernel,
        grid=(n // bn,),
        in_specs=[
            pl.BlockSpec((rows, d), lambda j: (0, 0)),
            pl.BlockSpec((d, bn), lambda j: (0, j)),
            pl.BlockSpec((1, bn), lambda j: (0, j)),
        ],
        out_specs=pl.BlockSpec((rows, bn), lambda j: (0, j)),
        out_shape=jax.ShapeDtypeStruct((rows, n), F32),
        compiler_params=pltpu.CompilerParams(
            dimension_semantics=("arbitrary",), vmem_limit_bytes=VMEM_LIMIT),
        name="adaln",
    )(cond, w, b)


def _pair_block_mask():
    r = lax.broadcasted_iota(jnp.int32, (PAIR_K, PAIR_V), 0) // DK
    c = lax.broadcasted_iota(jnp.int32, (PAIR_K, PAIR_V), 1) // DV
    return r == c


def _pair_states(full, mask):
    return [jnp.where(mask, full[p * PAIR_K:(p + 1) * PAIR_K, p * PAIR_V:(p + 1) * PAIR_V], 0.0)
            for p in range(PAIRS)]


def _ctx_kernel(ctx_ref, sh_ref, sc_ref, gpre_ref, w_ref, wg_ref, bg_ref, sf_ref, sb_ref):
    n = ctx_ref.shape[1]
    hc = _rms(ctx_ref[0], gpre_ref[...] * (1.0 + sc_ref[0])) + sh_ref[0]
    pc = _dot(hc.astype(BF16), w_ref[...])
    k = pc[:, C_K:C_K + QK]
    v = pc[:, C_V:C_V + GLA_W].astype(BF16)
    a = _dot(pc[:, C_A:C_A + LANES].astype(BF16), wg_ref[...]) + bg_ref[...]
    g = _log_sigmoid(a) * (1.0 / TAU)
    row = lax.broadcasted_iota(jnp.int32, (n, n), 0)
    col = lax.broadcasted_iota(jnp.int32, (n, n), 1)
    later = (col > row).astype(BF16)
    earlier = (col < row).astype(BF16)
    gf_hi, gf_lo = _split_bf16(g[:, :QK])
    gb_hi, gb_lo = _split_bf16(g[:, QK:])
    wf = jnp.exp(_dot(later, gf_hi) + _dot(later, gf_lo))
    wb = jnp.exp(_dot(earlier, gb_hi) + _dot(earlier, gb_lo))
    mask = _pair_block_mask()
    for p, s in enumerate(_pair_states(_dot_tn((k * wf).astype(BF16), v), mask)):
        sf_ref[0, p] = s
    for p, s in enumerate(_pair_states(_dot_tn((k * wb).astype(BF16), v), mask)):
        sb_ref[0, p] = s


def _ctx_states(ctx, mod3, g_pre, w_ctx, wg, bg):
    bsz, n, d = ctx.shape
    ctx_row = bsz * 6
    const = lambda shape: pl.BlockSpec(shape, lambda b: (0,) * len(shape))
    state = jax.ShapeDtypeStruct((bsz,) + STATE_SHAPE, F32)
    return pl.pallas_call(
        _ctx_kernel,
        grid=(bsz,),
        in_specs=[
            pl.BlockSpec((1, n, d), lambda b: (b, 0, 0)),
            pl.BlockSpec((1, 1, d), lambda b: (ctx_row + 0, 0, 0)),
            pl.BlockSpec((1, 1, d), lambda b: (ctx_row + 1, 0, 0)),
            const((1, d)),
            const((d, C_COLS)),
            const((LANES, 2 * QK)),
            const((1, 2 * QK)),
        ],
        out_specs=[pl.BlockSpec((1,) + STATE_SHAPE, lambda b: (b, 0, 0, 0))] * 2,
        out_shape=[state, state],
        compiler_params=pltpu.CompilerParams(
            dimension_semantics=("arbitrary",), vmem_limit_bytes=VMEM_LIMIT),
        name="ctx_state",
    )(ctx, mod3, mod3, g_pre, w_ctx, wg, bg)


def _gla_tile(p_ref, cols, g, state_ref, o_ref, reverse):
    col_k, col_q, col_v = cols
    n_chunks = p_ref.shape[0] // CHUNK
    ii = lax.broadcasted_iota(jnp.int32, (CHUNK, CHUNK), 0)
    jj = lax.broadcasted_iota(jnp.int32, (CHUNK, CHUNK), 1)
    i_all = lax.broadcasted_iota(jnp.int32, (CHUNK, HEADS * CHUNK), 0)
    j_all = lax.broadcasted_iota(jnp.int32, (CHUNK, HEADS * CHUNK), 1) % CHUNK
    if reverse:
        tri = (jj >= ii).astype(BF16)
        keep = j_all > i_all
        ref_row, last_row = CHUNK // 2, 0
    else:
        tri = (jj <= ii).astype(BF16)
        keep = j_all <= i_all
        ref_row, last_row = CHUNK // 2 - 1, CHUNK - 1
    tri2 = jnp.concatenate([tri, tri], axis=1)
    own_head = (lax.broadcasted_iota(jnp.int32, (HEADS * CHUNK, QK), 0) // CHUNK
                == lax.broadcasted_iota(jnp.int32, (HEADS * CHUNK, QK), 1) // DK)
    block = _pair_block_mask()

    chunks = range(n_chunks)
    order = range(n_chunks - 1, -1, -1) if reverse else range(n_chunks)
    rows = [slice(c * CHUNK, (c + 1) * CHUNK) for c in chunks]
    pair_k = [slice(p * PAIR_K, (p + 1) * PAIR_K) for p in range(PAIRS)]
    pair_v = [slice(p * PAIR_V, (p + 1) * PAIR_V) for p in range(PAIRS)]

    bs = []
    for c in chunks:
        g_hi, g_lo = _split_bf16(g[rows[c], :])
        bs.append(_dot(tri2, jnp.concatenate([g_hi, g_lo], axis=0)))

    q_ts, k_stacks, q_hs, k_hs, decays = [], [], [], [], []
    for c in chunks:
        b = bs[c]
        b_ref = b[ref_row:ref_row + 1, :]
        b_last = b[last_row:last_row + 1, :]
        e = b - b_ref
        q_t = p_ref[rows[c], col_q:col_q + QK] * (DK ** -0.5) * jnp.exp(e)
        k_t = p_ref[rows[c], col_k:col_k + QK] * jnp.exp(-e)
        q_ts.append(q_t.astype(BF16))
        k_stacks.append(
            jnp.where(own_head, jnp.concatenate([k_t] * HEADS, axis=0), 0.0).astype(BF16))
        q_hs.append((q_t * jnp.exp(b_ref)).astype(BF16))
        k_hs.append((k_t * jnp.exp(b_last - b_ref)).astype(BF16))
        decays.append(jnp.exp(b_last))

    atts = [jnp.where(keep, _dot_nt(q_ts[c], k_stacks[c]), 0.0).astype(BF16) for c in chunks]

    upds, decay_rows, v_diags = [], [], []
    for c in chunks:
        u_c, d_c, v_c = [], [], []
        for p in range(PAIRS):
            v_pair = p_ref[rows[c], col_v + p * PAIR_V:col_v + (p + 1) * PAIR_V].astype(F32)
            v_c.append(jnp.where(block, jnp.concatenate([v_pair, v_pair], axis=0), 0.0).astype(BF16))
            u_c.append(jnp.where(block, _dot_tn(k_hs[c][:, pair_k[p]], v_pair.astype(BF16)), 0.0))
            d_c.append(jnp.broadcast_to(decays[c][:, pair_k[p]], (PAIR_V, PAIR_K)).T)
        upds.append(u_c)
        decay_rows.append(d_c)
        v_diags.append(v_c)

    seen = [[None] * PAIRS for _ in chunks]
    for p in range(PAIRS):
        s = state_ref[p]
        for c in order:
            seen[c][p] = s.astype(BF16)
            s = s * decay_rows[c][p] + upds[c][p]
        state_ref[p] = s

    for c in chunks:
        for p in range(PAIRS):
            lhs = jnp.concatenate([atts[c][:, pair_k[p]], q_hs[c][:, pair_k[p]]], axis=1)
            rhs = jnp.concatenate([v_diags[c][p], seen[c][p]], axis=0)
            o_ref[rows[c], pair_v[p]] = _dot(lhs, rhs)


def _gate(a_low, wg_ref, bg_ref):
    a = _dot(a_low.astype(BF16), wg_ref[...]) + bg_ref[...]
    return _log_sigmoid(a) * (1.0 / TAU)


def _gla_bwd_kernel(x_ref, s0_ref, sh_ref, sc_ref, gpre_ref, w_ref, wg_ref, bg_ref,
                    ob_ref, kqv_ref, alow_ref, hx_ref, p_ref, state_ref):
    @pl.when(pl.program_id(1) == 0)
    def _():
        state_ref[...] = s0_ref[0]

    hx = (_rms(x_ref[0], gpre_ref[...] * (1.0 + sc_ref[0])) + sh_ref[0]).astype(BF16)
    hx_ref[0] = hx
    half = hx.shape[0] // 2
    p_ref[:, 0:B_A] = _dot(hx, w_ref[:, 0:B_A])
    p_ref[0:half, B_A:B_COLS] = _dot(hx[0:half], w_ref[:, B_A:B_COLS])
    p_ref[half:, B_A:B_COLS] = _dot(hx[half:], w_ref[:, B_A:B_COLS])
    kqv_ref[0] = p_ref[:, 0:B_A].astype(BF16)
    alow_ref[0] = p_ref[:, B_A:B_A + LANES]
    g = _gate(p_ref[:, B_A:B_A + LANES], wg_ref, bg_ref)
    _gla_tile(p_ref, (B_K, B_Q, B_V), g, state_ref, ob_ref.at[0], reverse=True)


def _gla_bwd(x, s_b, mod3, g_pre, w_bwd, wg_b, bg_b):
    bsz, t, d = x.shape
    nt = t // TILE_SCAN
    const = lambda shape: pl.BlockSpec(shape, lambda b, i: (0,) * len(shape),
                                       pipeline_mode=pl.Buffered(1))
    return pl.pallas_call(
        _gla_bwd_kernel,
        grid=(bsz, nt),
        in_specs=[
            pl.BlockSpec((1, TILE_SCAN, d), lambda b, i: (b, nt - 1 - i, 0)),
            pl.BlockSpec((1,) + STATE_SHAPE, lambda b, i: (b, 0, 0, 0)),
            pl.BlockSpec((1, 1, d), lambda b, i: (b * 6 + 0, 0, 0)),
            pl.BlockSpec((1, 1, d), lambda b, i: (b * 6 + 1, 0, 0)),
            const((1, d)),
            const((d, B_COLS)),
            const((LANES, QK)),
            const((1, QK)),
        ],
        out_specs=[pl.BlockSpec((1, TILE_SCAN, GLA_W), lambda b, i: (b, nt - 1 - i, 0)),
                   pl.BlockSpec((1, TILE_SCAN, B_A), lambda b, i: (b, nt - 1 - i, 0)),
                   pl.BlockSpec((1, TILE_SCAN, LANES), lambda b, i: (b, nt - 1 - i, 0)),
                   pl.BlockSpec((1, TILE_SCAN, d), lambda b, i: (b, nt - 1 - i, 0))],
        out_shape=[jax.ShapeDtypeStruct((bsz, t, GLA_W), F32),
                   jax.ShapeDtypeStruct((bsz, t, B_A), BF16),
                   jax.ShapeDtypeStruct((bsz, t, LANES), F32),
                   jax.ShapeDtypeStruct((bsz, t, d), BF16)],
        scratch_shapes=[pltpu.VMEM((TILE_SCAN, B_COLS), F32), pltpu.VMEM(STATE_SHAPE, F32)],
        compiler_params=pltpu.CompilerParams(
            dimension_semantics=("arbitrary", "arbitrary"), vmem_limit_bytes=VMEM_LIMIT),
        name="gla_bwd",
    )(x, s_b, mod3, mod3, g_pre, w_bwd, wg_b, bg_b)


def _mixer_kernel(x_ref, ob_ref, kqv_ref, alow_ref, hx_ref, s0_ref, gt_ref,
                  gpost_ref, w_ref, wg_ref, bg_ref, ghead_ref, wsc_ref, bsc_ref, wout_ref,
                  x1_ref, p_ref, o_ref, state_ref):
    @pl.when(pl.program_id(1) == 0)
    def _():
        state_ref[...] = s0_ref[0]

    n = x_ref.shape[1]
    x = x_ref[0]
    p_ref[...] = _dot(hx_ref[0], w_ref[...])
    g = _gate(alow_ref[0], wg_ref, bg_ref)
    _gla_tile(kqv_ref.at[0], (B_K, B_Q, B_V), g, state_ref, o_ref, reverse=False)

    heads = []
    for h in range(HEADS):
        cols = slice(h * DV, (h + 1) * DV)
        heads.append(_rms(o_ref[:, cols] + ob_ref[0, :, cols], ghead_ref[...]))
    o_gla = jnp.concatenate(heads, axis=1) * _silu(p_ref[:, M_OG:M_OG + GLA_W])

    u = p_ref[:, M_SC:M_SC + SC_W] * p_ref[:, M_SX:M_SX + SC_W]
    pos = lax.broadcasted_iota(jnp.int32, (n, 1), 0) % GRID_W
    left = jnp.where(pos == 0, 0.0, pltpu.roll(u, 1, 0))
    right = jnp.where(pos == GRID_W - 1, 0.0, pltpu.roll(u, n - 1, 0))
    conv = wsc_ref[0:1, :] * left + wsc_ref[1:2, :] * u + wsc_ref[2:3, :] * right + bsc_ref[...]
    o_sc = p_ref[:, M_SB:M_SB + SC_W] * conv

    yx = jnp.concatenate([o_gla, o_sc], axis=1).astype(BF16)
    x1_ref[0] = x + _rms(_dot(yx, wout_ref[...]), gpost_ref[...] * gt_ref[0])


def _mixer(x, o_b, kqv, a_low, hx, s_f, mod3, g_post, w_fwd, wg_f, bg_f, g_head, w_sc, b_sc,
           w_out):
    bsz, t, d = x.shape
    nt = t // TILE_SCAN
    const = lambda shape: pl.BlockSpec(shape, lambda b, i: (0,) * len(shape),
                                       pipeline_mode=pl.Buffered(1))
    mod = lambda j: pl.BlockSpec((1, 1, d), lambda b, i: (b * 6 + j, 0, 0))
    return pl.pallas_call(
        _mixer_kernel,
        grid=(bsz, nt),
        in_specs=[
            pl.BlockSpec((1, TILE_SCAN, d), lambda b, i: (b, i, 0)),
            pl.BlockSpec((1, TILE_SCAN, GLA_W), lambda b, i: (b, i, 0)),
            pl.BlockSpec((1, TILE_SCAN, B_A), lambda b, i: (b, i, 0)),
            pl.BlockSpec((1, TILE_SCAN, LANES), lambda b, i: (b, i, 0)),
            pl.BlockSpec((1, TILE_SCAN, d), lambda b, i: (b, i, 0)),
            pl.BlockSpec((1,) + STATE_SHAPE, lambda b, i: (b, 0, 0, 0)),
            mod(2),
            const((1, d)),
            const((d, M_COLS)),
            const((LANES, QK)), const((1, QK)),
            const((1, DV)),
            const((3, SC_W)), const((1, SC_W)),
            const((d, d)),
        ],
        out_specs=pl.BlockSpec((1, TILE_SCAN, d), lambda b, i: (b, i, 0)),
        out_shape=jax.ShapeDtypeStruct((bsz, t, d), F32),
        scratch_shapes=[pltpu.VMEM((TILE_SCAN, M_COLS), F32), pltpu.VMEM((TILE_SCAN, GLA_W), F32),
                        pltpu.VMEM(STATE_SHAPE, F32)],
        compiler_params=pltpu.CompilerParams(
            dimension_semantics=("arbitrary", "arbitrary"), vmem_limit_bytes=VMEM_LIMIT),
        name="mixer",
    )(x, o_b, kqv, a_low, hx, s_f, mod3, g_post, w_fwd, wg_f, bg_f, g_head, w_sc, b_sc, w_out)


def _ffn_kernel(xm_ref, xp_ref, xn_ref, sh_ref, sc_ref, gt_ref, gpre_ref, gpost_ref,
                wup_ref, wcf_ref, bcf_ref, wdown_ref, out_ref, hx_ref, h_ref):
    i = pl.program_id(1)
    n = xm_ref.shape[1]
    ext = n + 2 * GRID_W

    def prep(v):
        return (_rms(v, gpre_ref[...] * (1.0 + sc_ref[0])) + sh_ref[0]).astype(BF16)

    x1 = xm_ref[0]
    hx_ref[0:GRID_W, :] = prep(xp_ref[0])
    hx_ref[GRID_W:GRID_W + n, :] = prep(x1)
    hx_ref[GRID_W + n:ext, :] = prep(xn_ref[0])

    rowid = lax.broadcasted_iota(jnp.int32, (ext, 1), 0)
    has_prev = (i > 0).astype(F32)
    has_next = (i < pl.num_programs(1) - 1).astype(F32)
    pos = rowid % GRID_W

    for c in range(D_FF // MXU_N):
        cs = slice(c * MXU_N, (c + 1) * MXU_N)
        gs = slice(D_FF + c * MXU_N, D_FF + (c + 1) * MXU_N)
        u = _dot(hx_ref[...], wup_ref[:, cs])
        u = jnp.concatenate([u[0:GRID_W] * has_prev, u[GRID_W:GRID_W + n],
                             u[GRID_W + n:ext] * has_next], axis=0)
        gate = _dot(hx_ref[GRID_W:GRID_W + n, :], wup_ref[:, gs])
        u_l = jnp.where(pos == 0, 0.0, pltpu.roll(u, 1, 0)).astype(BF16)
        u_r = jnp.where(pos == GRID_W - 1, 0.0, pltpu.roll(u, ext - 1, 0)).astype(BF16)
        u_c = u.astype(BF16)
        w9 = wcf_ref[:, cs].astype(BF16)
        y = bcf_ref[:, cs].astype(BF16)
        for dr in range(3):
            rs = slice(dr * GRID_W, dr * GRID_W + n)
            y = (y + w9[3 * dr:3 * dr + 1] * u_l[rs]
                 + w9[3 * dr + 1:3 * dr + 2] * u_c[rs]
                 + w9[3 * dr + 2:3 * dr + 3] * u_r[rs])
        h_ref[:, cs] = (_silu(y.astype(F32)) * gate).astype(BF16)

    z = _dot(h_ref[...], wdown_ref[...])
    out_ref[0] = x1 + _rms(z, gpost_ref[...] * gt_ref[0])


def _ffn(x1, mod3, g_pre, g_post, w_up, w_cf, b_cf, w_down):
    bsz, t, d = x1.shape
    nt = t // TILE_FFN
    rows_per_tile = TILE_FFN // GRID_W
    n_rows = t // GRID_W
    const = lambda shape: pl.BlockSpec(shape, lambda b, i: (0,) * len(shape),
                                       pipeline_mode=pl.Buffered(1))
    mod = lambda j: pl.BlockSpec((1, 1, d), lambda b, i: (b * 6 + j, 0, 0))
    return pl.pallas_call(
        _ffn_kernel,
        grid=(bsz, nt),
        in_specs=[
            pl.BlockSpec((1, TILE_FFN, d), lambda b, i: (b, i, 0)),
            pl.BlockSpec((1, GRID_W, d),
                         lambda b, i: (b, jnp.maximum(i * rows_per_tile - 1, 0), 0)),
            pl.BlockSpec((1, GRID_W, d),
                         lambda b, i: (b, jnp.minimum((i + 1) * rows_per_tile, n_rows - 1), 0)),
            mod(3), mod(4), mod(5),
            const((1, d)), const((1, d)),
            const((d, 2 * D_FF)),
            const((9, D_FF)), const((1, D_FF)),
            const((D_FF, d)),
        ],
        out_specs=pl.BlockSpec((1, TILE_FFN, d), lambda b, i: (b, i, 0)),
        out_shape=jax.ShapeDtypeStruct((bsz, t, d), F32),
        scratch_shapes=[pltpu.VMEM((TILE_FFN + 2 * GRID_W, d), BF16),
                        pltpu.VMEM((TILE_FFN, D_FF), BF16)],
        compiler_params=pltpu.CompilerParams(
            dimension_semantics=("arbitrary", "arbitrary"), vmem_limit_bytes=VMEM_LIMIT),
        name="ffn",
    )(x1, x1, x1, mod3, mod3, mod3, g_pre, g_post, w_up, w_cf, b_cf, w_down)


def kernel(x, c, ctx, c_ctx, w_ada, b_ada, g_pre_mix, g_post_mix, g_pre_ffn, g_post_ffn,
           w_in, w_af, b_af, w_ab, b_ab, g_head, w_sc, b_sc, w_out, w_up, w_cf, b_cf, w_down):
    bsz, t, d = x.shape
    assert d == D_MODEL and t % TILE_SCAN == 0 and t % TILE_FFN == 0 and w_in.shape[0] == 1

    cond = jnp.concatenate([c, c_ctx[None, :], jnp.zeros((16 - bsz - 1, d), F32)], axis=0)
    mod = _adaln(cond, w_ada[0], b_ada[0][None, :])
    mod3 = mod.reshape(16 * 6, 1, d)

    wi = w_in[0]
    sl = lambda a, b: wi[:, a:b]
    gate_cols = jnp.concatenate(
        [sl(COL_AF, COL_Q), jnp.zeros((d, GATE_PAD - 2 * LOWRANK), F32)], axis=1)
    w_fwd = sl(COL_OG, D_IN).astype(BF16)
    w_bwd = jnp.concatenate(
        [sl(COL_K, COL_V), sl(COL_Q, COL_OG), sl(COL_V, COL_AF), gate_cols], axis=1).astype(BF16)
    w_ctx = jnp.concatenate([sl(COL_K, COL_V), sl(COL_V, COL_AF), gate_cols], axis=1).astype(BF16)
    wg = jnp.zeros((LANES, 2 * QK), F32)
    wg = wg.at[0:LOWRANK, 0:QK].set(w_af[0]).at[LOWRANK:2 * LOWRANK, QK:].set(w_ab[0]).astype(BF16)
    bg = jnp.concatenate([b_af[0], b_ab[0]])[None, :]

    s_f, s_b = _ctx_states(ctx, mod3, g_pre_mix, w_ctx, wg, bg)
    o_b, kqv, a_low, hx = _gla_bwd(x, s_b, mod3, g_pre_mix, w_bwd, wg[:, QK:], bg[:, QK:])
    x1 = _mixer(x, o_b, kqv, a_low, hx, s_f, mod3, g_post_mix, w_fwd, wg[:, :QK],
                bg[:, :QK], g_head, w_sc[0], b_sc, w_out[0].astype(BF16))
    return _ffn(x1, mod3, g_pre_ffn, g_post_ffn, w_up[0].astype(BF16),
                w_cf[0].reshape(9, D_FF), b_cf, w_down[0].astype(BF16))
```
